```python
import math
import jax, jax.numpy as jnp
from jax import lax
import numpy as np

D_MODEL = 1024
BATCH = 32
SEQ = 256
DEPTH = 4
DEC_BATCH = 2
DEC_SEQ = 4096
PAST_LEN = 512

GRID_W = 64
EPS = 1e-6
Q_BLOCK = 128
ROPE_BASE = 10000.0

CHUNK = 128
SG_WIDTH = 512
SG_GROUPS = 8
SG_GROUP_DIM = SG_WIDTH // SG_GROUPS
DIFF_HEADS = 4
DIFF_HEAD_DIM = 64
DIFF_WIDTH = DIFF_HEADS * 2 * DIFF_HEAD_DIM
NA_HEADS = 8
NA_HEAD_DIM = 64
NA_WIDTH = NA_HEADS * NA_HEAD_DIM
NA_WIN_R = 8
NA_WIN_C = 16
FNET_GROUPS = 4
FNET_WIDTH = D_MODEL
FNET_GROUP_DIM = FNET_WIDTH // FNET_GROUPS

N_BRANCH = 4
IN_SIZES = (SG_WIDTH, SG_WIDTH, SG_WIDTH,
            DIFF_WIDTH, DIFF_WIDTH, DIFF_WIDTH, DIFF_WIDTH,
            NA_WIDTH, NA_WIDTH, NA_WIDTH, NA_WIDTH,
            FNET_WIDTH, N_BRANCH * D_MODEL)
IN_WIDTH = sum(IN_SIZES)
IN_SPLITS = tuple(int(v) for v in np.cumsum(IN_SIZES)[:-1])

kernel_name = "hybrid_diffusion_gated_branches_step"


def rms_norm(x, g):
    xf = x.astype(jnp.float32)
    y = xf * lax.rsqrt(jnp.mean(xf * xf, axis=-1, keepdims=True) + EPS)
    return (y * g.astype(jnp.float32)).astype(x.dtype)


def axial_rope(x, n_tokens):
    dh = x.shape[-1]
    half = dh // 2
    nf = half // 2
    t = jnp.arange(n_tokens)
    row = (t // GRID_W).astype(jnp.float32)
    col = (t % GRID_W).astype(jnp.float32)
    inv = ROPE_BASE ** (-jnp.arange(nf, dtype=jnp.float32) / nf)

    def rot(xp, pos):
        ang = pos[:, None] * inv[None, :]
        cos = jnp.cos(ang)[None, :, None, None, :]
        sin = jnp.sin(ang)[None, :, None, None, :]
        x1, x2 = xp[..., :nf], xp[..., nf:]
        return jnp.concatenate([x1 * cos - x2 * sin, x2 * cos + x1 * sin], axis=-1)

    xf = x.astype(jnp.float32)
    return jnp.concatenate([rot(xf[..., :half], row), rot(xf[..., half:], col)], axis=-1).astype(x.dtype)


def _query_blocks(q):
    b, sq = q.shape[:2]
    return jnp.moveaxis(q.reshape((b, sq // Q_BLOCK, Q_BLOCK) + q.shape[2:]), 1, 0)


def _merge_blocks(o):
    o = jnp.moveaxis(o, 0, 1)
    return o.reshape((o.shape[0], o.shape[1] * o.shape[2]) + o.shape[3:])


def diff_attention(q, k, v, lam):
    scale = DIFF_HEAD_DIM ** -0.5

    def block(qb):
        s = jnp.einsum('bqhmd,bkhmd->bhmqk', qb, k, preferred_element_type=jnp.float32) * scale
        p = jax.nn.softmax(s, axis=-1)
        a = p[:, :, 0] - lam * p[:, :, 1]
        return jnp.einsum('bhqk,bkhe->bqhe', a.astype(v.dtype), v)

    return _merge_blocks(lax.map(block, _query_blocks(q)))


def context_attention(q, k, v):
    scale = q.shape[-1] ** -0.5

    def block(qb):
        s = jnp.einsum('bqhd,bkhd->bhqk', qb, k, preferred_element_type=jnp.float32) * scale
        p = jax.nn.softmax(s, axis=-1).astype(v.dtype)
        return jnp.einsum('bhqk,bkhd->bqhd', p, v)

    return _merge_blocks(lax.map(block, _query_blocks(q)))


def neighbourhood_attention(q, k, v, ck, cv, rpb):
    b, s, h, dh = q.shape
    rows = s // GRID_W
    kr = min(NA_WIN_R, rows)
    scale = dh ** -0.5
    r = jnp.arange(rows)
    r0 = jnp.clip(r - kr // 2, 0, rows - kr)
    row_idx = r0[:, None] + jnp.arange(kr)[None, :]
    c = jnp.arange(GRID_W)
    c0 = jnp.clip(c - NA_WIN_C // 2, 0, GRID_W - NA_WIN_C)
    in_win = (c[None, :] >= c0[:, None]) & (c[None, :] < c0[:, None] + NA_WIN_C)

    qg = q.reshape(b, rows, GRID_W, h, dh)
    kg = k.reshape(b, rows, GRID_W, h, dh)[:, row_idx]
    vg = v.reshape(b, rows, GRID_W, h, dh)[:, row_idx]

    s_loc = jnp.einsum('brqhd,brjkhd->brhqjk', qg, kg, preferred_element_type=jnp.float32) * scale
    dr = row_idx - r[:, None]
    dc = jnp.clip(c[None, :] - c[:, None], -(NA_WIN_C - 1), NA_WIN_C - 1)
    bias = rpb[:, dr[:, :, None, None] + NA_WIN_R - 1, dc[None, None] + NA_WIN_C - 1]
    bias = jnp.transpose(bias, (1, 0, 3, 2, 4)).astype(jnp.float32)
    s_loc = jnp.where(in_win[None, None, None, :, None, :], s_loc + bias[None], -jnp.inf)
    s_loc = s_loc.reshape(b, rows, h, GRID_W, kr * GRID_W)

    s_ctx = jnp.einsum('brqhd,bkhd->brhqk', qg, ck, preferred_element_type=jnp.float32) * scale
    p = jax.nn.softmax(jnp.concatenate([s_loc, s_ctx], axis=-1), axis=-1).astype(v.dtype)
    p_loc = p[..., :kr * GRID_W].reshape(b, rows, h, GRID_W, kr, GRID_W)
    p_ctx = p[..., kr * GRID_W:]
    out = (jnp.einsum('brhqjk,brjkhd->brqhd', p_loc, vg)
           + jnp.einsum('brhqk,bkhd->brqhd', p_ctx, cv))
    return out.reshape(b, s, h, dh)


def spatial_gating(u, v, g_v, w_s, b_s):
    b, l, _ = v.shape
    vn = rms_norm(v, g_v).reshape(b, l // CHUNK, CHUNK, SG_GROUPS, SG_GROUP_DIM)
    s = jnp.einsum('gpq,bnqgc->bnpgc', w_s, vn) + b_s.T[None, None, :, :, None]
    return u * s.reshape(b, l, SG_WIDTH)


def fourier_mix(h):
    b, l, _ = h.shape
    hg = h.astype(jnp.float32).reshape(b, l, FNET_GROUPS, FNET_GROUP_DIM)
    f = jnp.fft.fft2(hg, axes=(1, 3), norm='ortho').real
    return f.reshape(b, l, FNET_WIDTH).astype(h.dtype)


def trunk_layer(x, cond, l, lp, ctx):
    (w_mod, b_mod, g_pre, g_post, w_in, sg_norm_g, sg_w, sg_b, lq1, lk1, lq2, lk2, subln_g, rpb,
     p_sg, p_diff, p_na, p_fnet, w_o) = lp
    b, s, _ = x.shape
    shift, scale, gate = jnp.split(jax.nn.silu(cond) @ w_mod + b_mod, 3, axis=-1)
    h = rms_norm(x, g_pre) * (1 + scale) + shift
    (sg_u, sg_v, sg_gate, dq, dk, dv, d_gate, nq, nk, nv, n_gate, f_gate, merge) = jnp.split(
        h @ w_in, IN_SPLITS, axis=-1)

    y_sg = spatial_gating(sg_u, sg_v, sg_norm_g, sg_w, sg_b)

    dq = dq.reshape(b, s, DIFF_HEADS, 2, DIFF_HEAD_DIM)
    dk = dk.reshape(b, s, DIFF_HEADS, 2, DIFF_HEAD_DIM)
    dv = dv.reshape(b, s, DIFF_HEADS, 2 * DIFF_HEAD_DIM)
    nq = nq.reshape(b, s, NA_HEADS, NA_HEAD_DIM)
    nk = nk.reshape(b, s, NA_HEADS, NA_HEAD_DIM)
    nv = nv.reshape(b, s, NA_HEADS, NA_HEAD_DIM)
    lam_init = 0.8 - 0.6 * math.exp(-0.3 * l)
    lam = (jnp.exp(jnp.sum(lq1.astype(jnp.float32) * lk1.astype(jnp.float32)))
           - jnp.exp(jnp.sum(lq2.astype(jnp.float32) * lk2.astype(jnp.float32))) + lam_init)

    if ctx is None:
        new_ctx = (dk.reshape(b, s, DIFF_HEADS, 2 * DIFF_HEAD_DIM), dv, nk, nv)
        y_diff = diff_attention(dq, dk, dv, lam)
        y_na = context_attention(nq, nk, nv)
    else:
        ck_d, cv_d, ck_n, cv_n = ctx
        n_ctx = ck_d.shape[1]
        dq = axial_rope(dq, s)
        dk = axial_rope(dk, s)
        k_all = jnp.concatenate([dk, ck_d.reshape(b, n_ctx, DIFF_HEADS, 2, DIFF_HEAD_DIM)], axis=1)
        v_all = jnp.concatenate([dv, cv_d], axis=1)
        y_diff = diff_attention(dq, k_all, v_all, lam)
        y_na = neighbourhood_attention(nq, nk, nv, ck_n, cv_n, rpb)
        new_ctx = None

    y_diff = (rms_norm(y_diff, subln_g) * (1 - lam_init)).reshape(b, s, DIFF_WIDTH)
    y_na = y_na.reshape(b, s, NA_WIDTH)
    y_f = fourier_mix(h)

    g = jax.nn.sigmoid(merge.reshape(b, s, N_BRANCH, D_MODEL))
    mixed = (g[..., 0, :] * ((y_sg * jax.nn.silu(sg_gate)) @ p_sg)
             + g[..., 1, :] * ((y_diff * jax.nn.silu(d_gate)) @ p_diff)
             + g[..., 2, :] * ((y_na * jax.nn.silu(n_gate)) @ p_na)
             + g[..., 3, :] * ((y_f * jax.nn.silu(f_gate)) @ p_fnet))
    out = rms_norm(mixed @ w_o, g_post)
    return x + gate * out, new_ctx


def setup_inputs(seed: int = 0) -> dict:
    key = jax.random.key(seed)
    ks = jax.random.split(key, 32)
    D = D_MODEL

    def nrm(k, shape, s):
        return jax.random.normal(k, shape, jnp.float32) * s

    return {
        "x_prompt": nrm(ks[0], (BATCH, SEQ, D), 1.0),
        "x_sample": nrm(ks[1], (DEC_BATCH, DEC_SEQ, D), 1.0),
        "cache_diff_k": nrm(ks[2], (DEC_BATCH, DEPTH, PAST_LEN, DIFF_HEADS, 2 * DIFF_HEAD_DIM), 1.0),
        "cache_diff_v": nrm(ks[3], (DEC_BATCH, DEPTH, PAST_LEN, DIFF_HEADS, 2 * DIFF_HEAD_DIM), 1.0),
        "cache_na_k": nrm(ks[4], (DEC_BATCH, DEPTH, PAST_LEN, NA_HEADS, NA_HEAD_DIM), 1.0),
        "cache_na_v": nrm(ks[5], (DEC_BATCH, DEPTH, PAST_LEN, NA_HEADS, NA_HEAD_DIM), 1.0),
        "c": nrm(ks[6], (DEC_BATCH, D), 1.0),
        "c_ctx": nrm(ks[7], (D,), 1.0),
        "w_mod": nrm(ks[8], (DEPTH, D, 3 * D), 0.5 * D ** -0.5),
        "b_mod": nrm(ks[9], (DEPTH, 3 * D), 0.02),
        "g_pre": 1.0 + nrm(ks[10], (DEPTH, D), 0.02),
        "g_post": 1.0 + nrm(ks[11], (DEPTH, D), 0.02),
        "w_in": nrm(ks[12], (DEPTH, D, IN_WIDTH), D ** -0.5),
        "sg_norm_g": 1.0 + nrm(ks[13], (DEPTH, SG_WIDTH), 0.02),
        "sg_w": nrm(ks[14], (DEPTH, SG_GROUPS, CHUNK, CHUNK), CHUNK ** -0.5),
        "sg_b": 1.0 + nrm(ks[15], (DEPTH, SG_GROUPS, CHUNK), 0.1),
        "diff_lam_q1": nrm(ks[16], (DEPTH, DIFF_HEAD_DIM), 0.1),
        "diff_lam_k1": nrm(ks[17], (DEPTH, DIFF_HEAD_DIM), 0.1),
        "diff_lam_q2": nrm(ks[18], (DEPTH, DIFF_HEAD_DIM), 0.1),
        "diff_lam_k2": nrm(ks[19], (DEPTH, DIFF_HEAD_DIM), 0.1),
        "diff_subln_g": 1.0 + nrm(ks[20], (DEPTH, 2 * DIFF_HEAD_DIM), 0.02),
        "na_rpb": nrm(ks[21], (DEPTH, NA_HEADS, 2 * NA_WIN_R - 1, 2 * NA_WIN_C - 1), 0.02),
        "w_proj_sg": nrm(ks[22], (DEPTH, SG_WIDTH, D), SG_WIDTH ** -0.5),
        "w_proj_diff": nrm(ks[23], (DEPTH, DIFF_WIDTH, D), DIFF_WIDTH ** -0.5),
        "w_proj_na": nrm(ks[24], (DEPTH, NA_WIDTH, D), NA_WIDTH ** -0.5),
        "w_proj_fnet": nrm(ks[25], (DEPTH, FNET_WIDTH, D), FNET_WIDTH ** -0.5),
        "w_out": nrm(ks[26], (DEPTH, D, D), D ** -0.5),
    }


def reference(x_prompt, x_sample, cache_diff_k, cache_diff_v, cache_na_k, cache_na_v, c, c_ctx,
              w_mod, b_mod, g_pre, g_post, w_in, sg_norm_g, sg_w, sg_b,
              diff_lam_q1, diff_lam_k1, diff_lam_q2, diff_lam_k2, diff_subln_g, na_rpb,
              w_proj_sg, w_proj_diff, w_proj_na, w_proj_fnet, w_out):
    y_p = x_prompt
    y_s = x_sample
    cond_s = c[:, None, :]
    dk_list, dv_list, nk_list, nv_list = [], [], [], []
    for l in range(DEPTH):
        lp = (w_mod[l], b_mod[l], g_pre[l], g_post[l], w_in[l], sg_norm_g[l], sg_w[l], sg_b[l],
              diff_lam_q1[l], diff_lam_k1[l], diff_lam_q2[l], diff_lam_k2[l], diff_subln_g[l], na_rpb[l],
              w_proj_sg[l], w_proj_diff[l], w_proj_na[l], w_proj_fnet[l], w_out[l])
        y_p, (ndk, ndv, nnk, nnv) = trunk_layer(y_p, c_ctx, l, lp, None)
        dk_list.append(ndk)
        dv_list.append(ndv)
        nk_list.append(nnk)
        nv_list.append(nnv)
        y_s, _ = trunk_layer(y_s, cond_s, l, lp,
                             (cache_diff_k[:, l], cache_diff_v[:, l], cache_na_k[:, l], cache_na_v[:, l]))
    new_diff_k = jnp.stack(dk_list, axis=1)
    new_diff_v = jnp.stack(dv_list, axis=1)
    new_na_k = jnp.stack(nk_list, axis=1)
    new_na_v = jnp.stack(nv_list, axis=1)
    return (y_p, y_s, new_diff_k, new_diff_v, new_na_k, new_na_v)
```

```python
import functools
import math

import numpy as np
import jax
import jax.numpy as jnp
from jax import lax
from jax.experimental import pallas as pl
from jax.experimental.pallas import tpu as pltpu

F32 = jnp.float32
BF16 = jnp.bfloat16

D_MODEL = 1024
DEPTH = 4
GRID_W = 64
GRID_H = 64
EPS = 1e-6
ROPE_BASE = 10000.0
CHUNK = 128
SG_WIDTH = 512
SG_GROUPS = 8
DIFF_HEADS = 4
DIFF_HEAD_DIM = 64
NA_HEADS = 8
NA_HEAD_DIM = 64
NA_WIN_R = 8
NA_WIN_C = 16
FNET_GROUPS = 4
FNET_GROUP_DIM = 256
N_BRANCH = 4
LANES = 128
VMEM_LIMIT = 56 * 1024 * 1024

_REF_SEGMENTS = (("sg_u", 512), ("sg_v", 512), ("sg_gate", 512),
                 ("dq", 512), ("dk", 512), ("dv", 512), ("d_gate", 512),
                 ("nq", 512), ("nk", 512), ("nv", 512), ("n_gate", 512),
                 ("f_gate", 1024), ("merge", 4096))
_MY_ORDER = ("merge", "f_gate", "sg_u", "sg_v", "sg_gate", "dq", "dk", "dv", "d_gate",
             "nq", "nk", "nv", "n_gate")


def _layout():
    ref_off, o = {}, 0
    for name, width in _REF_SEGMENTS:
        ref_off[name] = (o, width)
        o += width
    my_off, o = {}, 0
    for name in _MY_ORDER:
        my_off[name] = o
        o += ref_off[name][1]
    return ref_off, my_off, o


_REF_OFF, _OFF, IN_WIDTH = _layout()
TN = 512
N_COL_TILES = IN_WIDTH // TN


def _silu(x):
    return x * jax.nn.sigmoid(x)


def _params(sem):
    return pltpu.CompilerParams(dimension_semantics=sem, vmem_limit_bytes=VMEM_LIMIT)


def _mod_kernel(cond_ref, w_ref, b_ref, o_ref):
    s = _silu(cond_ref[...])
    o_ref[...] = jnp.dot(s, w_ref[...], precision=lax.Precision.HIGHEST,
                         preferred_element_type=F32) + b_ref[...]


def _modulation(cond8, w_mod, b_mod):
    b3 = b_mod.reshape(DEPTH, 1, 3 * D_MODEL)
    return pl.pallas_call(
        _mod_kernel,
        grid=(DEPTH, 3),
        in_specs=[pl.BlockSpec((8, D_MODEL), lambda l, j: (0, 0)),
                  pl.BlockSpec((None, D_MODEL, D_MODEL), lambda l, j: (l, 0, j)),
                  pl.BlockSpec((None, 1, D_MODEL), lambda l, j: (l, 0, j))],
        out_specs=pl.BlockSpec((None, 8, D_MODEL), lambda l, j: (l, 0, j)),
        out_shape=jax.ShapeDtypeStruct((DEPTH, 8, 3 * D_MODEL), F32),
        compiler_params=_params(("arbitrary", "arbitrary")),
    )(cond8, w_mod, b3)


def _rope(acc, cos_ref, sina_ref, sinb_ref):
    cos, sina, sinb = cos_ref[...], sina_ref[...], sinb_ref[...]
    outs = []
    for hd in range(acc.shape[1] // LANES):
        a = acc[:, hd * LANES:(hd + 1) * LANES]
        up = pltpu.roll(a, LANES - 16, 1)
        dn = pltpu.roll(a, 16, 1)
        outs.append(a * cos + up * sina + dn * sinb)
    return jnp.concatenate(outs, axis=1)


def _inproj_kernel(*refs, latent, tm, tiles_per_batch):
    if latent:
        (x_ref, mod_ref, gpre_ref, w_ref, cos_ref, sina_ref, sinb_ref, p_ref, h_ref, hb_ref) = refs
    else:
        (x_ref, mod_ref, gpre_ref, w_ref, p_ref, h_ref, dk_ref, dv_ref, nk_ref, nv_ref, hb_ref) = refs
    i = pl.program_id(0)
    j = pl.program_id(1)

    @pl.when(j == 0)
    def _():
        x = x_ref[...]
        r = lax.rsqrt(jnp.mean(x * x, axis=-1, keepdims=True) + EPS)
        row = (1 + i // tiles_per_batch) if latent else 0
        m = mod_ref[pl.ds(row, 1), :]
        shift = m[:, :D_MODEL]
        scale = m[:, D_MODEL:2 * D_MODEL]
        h = (x * r * gpre_ref[...]) * (1.0 + scale) + shift
        h_ref[...] = h
        hb_ref[...] = h.astype(BF16)

    acc = jnp.dot(hb_ref[...], w_ref[...], preferred_element_type=F32)
    j_dq, j_dk, j_dv = _OFF["dq"] // TN, _OFF["dk"] // TN, _OFF["dv"] // TN
    j_nq, j_nk, j_nv = _OFF["nq"] // TN, _OFF["nk"] // TN, _OFF["nv"] // TN
    qscale = DIFF_HEAD_DIM ** -0.5

    @pl.when(j == j_dq)
    def _():
        a = acc * qscale
        if latent:
            a = _rope(a, cos_ref, sina_ref, sinb_ref)
        p_ref[...] = a.astype(BF16)

    @pl.when(j == j_dk)
    def _():
        if latent:
            p_ref[...] = _rope(acc, cos_ref, sina_ref, sinb_ref).astype(BF16)
        else:
            dk_ref[...] = acc
            p_ref[...] = acc.astype(BF16)

    @pl.when(j == j_nq)
    def _():
        p_ref[...] = (acc * (NA_HEAD_DIM ** -0.5)).astype(BF16)

    if latent:
        plain = jnp.logical_and(jnp.logical_and(j != j_dq, j != j_dk), j != j_nq)
    else:
        for jj, ref in ((j_dv, dv_ref), (j_nk, nk_ref), (j_nv, nv_ref)):
            @pl.when(j == jj)
            def _(ref=ref):
                ref[...] = acc
                p_ref[...] = acc.astype(BF16)
        plain = jnp.logical_and(
            jnp.logical_and(jnp.logical_and(j != j_dq, j != j_dk), jnp.logical_and(j != j_nq, j != j_dv)),
            jnp.logical_and(j != j_nk, j != j_nv))

    @pl.when(plain)
    def _():
        p_ref[...] = acc.astype(BF16)


def _inproj(x2d, mod_all, g_pre3, w_in_b, l, *, latent, rope_tabs=None, tm=512, seq=None):
    t = x2d.shape[0]
    n_i = t // tm
    in_specs = [pl.BlockSpec((tm, D_MODEL), lambda i, j: (i, 0)),
                pl.BlockSpec((None, 8, 3 * D_MODEL), lambda i, j: (l, 0, 0)),
                pl.BlockSpec((None, 1, D_MODEL), lambda i, j: (l, 0, 0)),
                pl.BlockSpec((None, D_MODEL, TN), lambda i, j: (l, 0, j))]
    args = [x2d, mod_all, g_pre3, w_in_b]
    out_specs = [pl.BlockSpec((tm, TN), lambda i, j: (i, j)),
                 pl.BlockSpec((tm, D_MODEL), lambda i, j: (i, 0))]
    out_shape = [jax.ShapeDtypeStruct((t, IN_WIDTH), BF16),
                 jax.ShapeDtypeStruct((t, D_MODEL), F32)]
    tiles_per_batch = 1
    if latent:
        tiles_per_batch = seq // tm
        for tab in rope_tabs:
            in_specs.append(pl.BlockSpec((tm, LANES), lambda i, j: (i % tiles_per_batch, 0)))
            args.append(tab)
    else:
        for _ in range(4):
            out_specs.append(pl.BlockSpec((tm, TN), lambda i, j: (i, 0)))
            out_shape.append(jax.ShapeDtypeStruct((t, TN), F32))
    return pl.pallas_call(
        functools.partial(_inproj_kernel, latent=latent, tm=tm, tiles_per_batch=tiles_per_batch),
        grid=(n_i, N_COL_TILES),
        in_specs=in_specs,
        out_specs=out_specs,
        out_shape=out_shape,
        scratch_shapes=[pltpu.VMEM((tm, D_MODEL), BF16)],
        compiler_params=_params(("arbitrary", "arbitrary")),
    )(*args)


def _softmax_rows(s):
    m = jnp.max(s, axis=-1, keepdims=True)
    e = jnp.exp(s - m)
    return e * (1.0 / jnp.sum(e, axis=-1, keepdims=True))


def _diff_attn_kernel(*refs, lam_init, n_self, has_ctx):
    if has_ctx:
        (lamp_ref, subg_ref, q_ref, k_ref, v_ref, gate_ref, ck_ref, cv_ref, o_ref, kc_ref, vc_ref) = refs

        @pl.when(pl.program_id(2) == 0)
        def _():
            kc_ref[0:n_self, :] = k_ref[...]
            vc_ref[0:n_self, :] = v_ref[...]
            kc_ref[n_self:, :] = ck_ref[...].astype(BF16)
            vc_ref[n_self:, :] = cv_ref[...].astype(BF16)

        k = kc_ref[...]
        v = vc_ref[...]
    else:
        (lamp_ref, subg_ref, q_ref, k_ref, v_ref, gate_ref, o_ref) = refs
        k = k_ref[...]
        v = v_ref[...]
    q = q_ref[...]
    lane = lax.broadcasted_iota(jnp.int32, q.shape, 1)
    zero = jnp.zeros_like(q)
    q1 = jnp.where(lane < DIFF_HEAD_DIM, q, zero)
    q2 = jnp.where(lane >= DIFF_HEAD_DIM, q, zero)
    dn = (((1,), (1,)), ((), ()))
    p1 = _softmax_rows(lax.dot_general(q1, k, dn, preferred_element_type=F32))
    p2 = _softmax_rows(lax.dot_general(q2, k, dn, preferred_element_type=F32))
    lp = lamp_ref[...]
    lam = (jnp.exp(jnp.sum(lp[0:1] * lp[1:2], axis=-1, keepdims=True))
           - jnp.exp(jnp.sum(lp[2:3] * lp[3:4], axis=-1, keepdims=True)) + lam_init)
    a = (p1 - lam * p2).astype(BF16)
    o = jnp.dot(a, v, preferred_element_type=F32)
    y = o * lax.rsqrt(jnp.mean(o * o, axis=-1, keepdims=True) + EPS) * subg_ref[...]
    y = y * (1.0 - lam_init)
    o_ref[...] = (y * _silu(gate_ref[...].astype(F32))).astype(BF16)


def _diff_attention(p3, lamp, subg3, l, *, tq, ctx=None):
    b, s, _ = p3.shape
    cq, ck, cv, cg = (_OFF[n] // LANES for n in ("dq", "dk", "dv", "d_gate"))
    lam_init = 0.8 - 0.6 * math.exp(-0.3 * l)
    in_specs = [pl.BlockSpec((None, 4, DIFF_HEAD_DIM), lambda bi, h, qi: (l, 0, 0)),
                pl.BlockSpec((None, 1, LANES), lambda bi, h, qi: (l, 0, 0)),
                pl.BlockSpec((None, tq, LANES), lambda bi, h, qi: (bi, qi, cq + h)),
                pl.BlockSpec((None, s, LANES), lambda bi, h, qi: (bi, 0, ck + h)),
                pl.BlockSpec((None, s, LANES), lambda bi, h, qi: (bi, 0, cv + h)),
                pl.BlockSpec((None, tq, LANES), lambda bi, h, qi: (bi, qi, cg + h))]
    args = [lamp, subg3, p3, p3, p3, p3]
    scratch = []
    if ctx is not None:
        ck4, cv4 = ctx
        n_ctx = ck4.shape[2]
        for arr in (ck4, cv4):
            in_specs.append(pl.BlockSpec((None, None, n_ctx, LANES), lambda bi, h, qi: (bi, l, 0, h)))
            args.append(arr)
        scratch = [pltpu.VMEM((s + n_ctx, LANES), BF16), pltpu.VMEM((s + n_ctx, LANES), BF16)]
    return pl.pallas_call(
        functools.partial(_diff_attn_kernel, lam_init=lam_init, n_self=s, has_ctx=ctx is not None),
        grid=(b, DIFF_HEADS, s // tq),
        in_specs=in_specs,
        out_specs=pl.BlockSpec((None, tq, LANES), lambda bi, h, qi: (bi, qi, h)),
        out_shape=jax.ShapeDtypeStruct((b, s, DIFF_HEADS * LANES), BF16),
        scratch_shapes=scratch,
        compiler_params=_params(("arbitrary", "arbitrary", "arbitrary")),
    )(*args)


def _ctx_attn_kernel(q_ref, k_ref, v_ref, gate_ref, o_ref):
    q, k, v = q_ref[...], k_ref[...], v_ref[...]
    lane = lax.broadcasted_iota(jnp.int32, q.shape, 1)
    zero = jnp.zeros_like(q)
    dn = (((1,), (1,)), ((), ()))
    outs = []
    for hh in range(2):
        sel = (lane < NA_HEAD_DIM) if hh == 0 else (lane >= NA_HEAD_DIM)
        qh = jnp.where(sel, q, zero)
        p = _softmax_rows(lax.dot_general(qh, k, dn, preferred_element_type=F32)).astype(BF16)
        outs.append(jnp.dot(p, v, preferred_element_type=F32))
    o = jnp.where(lane < NA_HEAD_DIM, outs[0], outs[1])
    o_ref[...] = (o * _silu(gate_ref[...].astype(F32))).astype(BF16)


def _context_attention(p3):
    b, s, _ = p3.shape
    cq, ck, cv, cg = (_OFF[n] // LANES for n in ("nq", "nk", "nv", "n_gate"))
    spec = lambda c: pl.BlockSpec((None, s, LANES), lambda bi, hp: (bi, 0, c + hp))
    return pl.pallas_call(
        _ctx_attn_kernel,
        grid=(b, NA_HEADS // 2),
        in_specs=[spec(cq), spec(ck), spec(cv), spec(cg)],
        out_specs=pl.BlockSpec((None, s, LANES), lambda bi, hp: (bi, 0, hp)),
        out_shape=jax.ShapeDtypeStruct((b, s, NA_HEADS * NA_HEAD_DIM), BF16),
        compiler_params=_params(("arbitrary", "arbitrary")),
    )(p3, p3, p3, p3)


def _na_attn_kernel(q_ref, k_ref, v_ref, gate_ref, ck_ref, cv_ref, bias_ref, o_ref, *, rows_per_step):
    rb = pl.program_id(2)
    ckb = ck_ref[...].astype(BF16)
    cvb = cv_ref[...].astype(BF16)
    n_loc = NA_WIN_R * GRID_W
    lane = lax.broadcasted_iota(jnp.int32, (GRID_W, LANES), 1)
    dn = (((1,), (1,)), ((), ()))

    def row_body(rl, carry):
        r = rb * rows_per_step + rl
        r0 = jnp.clip(r - NA_WIN_R // 2, 0, GRID_H - NA_WIN_R)
        start = r0 - r + (NA_WIN_R - 1)
        qoff = pl.multiple_of(rl * GRID_W, GRID_W)
        koff = pl.multiple_of(r0 * GRID_W, GRID_W)
        q = q_ref[pl.ds(qoff, GRID_W), :]
        kl = k_ref[pl.ds(koff, n_loc), :]
        vl = v_ref[pl.ds(koff, n_loc), :]
        zero = jnp.zeros_like(q)
        outs = []
        for hh in range(2):
            sel = (lane < NA_HEAD_DIM) if hh == 0 else (lane >= NA_HEAD_DIM)
            qh = jnp.where(sel, q, zero)
            s_loc = lax.dot_general(qh, kl, dn, preferred_element_type=F32) + bias_ref[start, hh]
            s_ctx = lax.dot_general(qh, ckb, dn, preferred_element_type=F32)
            m = jnp.maximum(jnp.max(s_loc, axis=-1, keepdims=True), jnp.max(s_ctx, axis=-1, keepdims=True))
            e_loc = jnp.exp(s_loc - m)
            e_ctx = jnp.exp(s_ctx - m)
            inv = 1.0 / (jnp.sum(e_loc, axis=-1, keepdims=True) + jnp.sum(e_ctx, axis=-1, keepdims=True))
            outs.append(jnp.dot((e_loc * inv).astype(BF16), vl, preferred_element_type=F32)
                        + jnp.dot((e_ctx * inv).astype(BF16), cvb, preferred_element_type=F32))
        o = jnp.where(lane < NA_HEAD_DIM, outs[0], outs[1])
        g = gate_ref[pl.ds(qoff, GRID_W), :].astype(F32)
        o_ref[pl.ds(qoff, GRID_W), :] = (o * _silu(g)).astype(BF16)
        return carry

    lax.fori_loop(0, rows_per_step, row_body, 0)


def _na_bias_table(rpb_l):
    c = np.arange(GRID_W)
    c0 = np.clip(c - NA_WIN_C // 2, 0, GRID_W - NA_WIN_C)
    in_win = (c[None, :] >= c0[:, None]) & (c[None, :] < c0[:, None] + NA_WIN_C)
    dc = np.clip(c[None, :] - c[:, None], -(NA_WIN_C - 1), NA_WIN_C - 1) + NA_WIN_C - 1
    dr = np.arange(NA_WIN_R)[:, None] + np.arange(NA_WIN_R)[None, :]
    g = rpb_l[:, dr[:, :, None, None], dc[None, None]]
    g = jnp.where(in_win[None, None, None], g.astype(F32), -1e30)
    g = jnp.transpose(g, (1, 0, 3, 2, 4))
    return g.reshape(NA_WIN_R, NA_HEADS, GRID_W, NA_WIN_R * GRID_W)


def _neighbourhood_attention(p3, ck4, cv4, bias, l, *, rows_per_step=8):
    b, s, _ = p3.shape
    cq, ck, cv, cg = (_OFF[n] // LANES for n in ("nq", "nk", "nv", "n_gate"))
    n_ctx = ck4.shape[2]
    tq = rows_per_step * GRID_W
    n_loc = NA_WIN_R * GRID_W
    return pl.pallas_call(
        functools.partial(_na_attn_kernel, rows_per_step=rows_per_step),
        grid=(b, NA_HEADS // 2, GRID_H // rows_per_step),
        in_specs=[pl.BlockSpec((None, tq, LANES), lambda bi, hp, rb: (bi, rb, cq + hp)),
                  pl.BlockSpec((None, s, LANES), lambda bi, hp, rb: (bi, 0, ck + hp)),
                  pl.BlockSpec((None, s, LANES), lambda bi, hp, rb: (bi, 0, cv + hp)),
                  pl.BlockSpec((None, tq, LANES), lambda bi, hp, rb: (bi, rb, cg + hp)),
                  pl.BlockSpec((None, None, n_ctx, LANES), lambda bi, hp, rb: (bi, l, 0, hp)),
                  pl.BlockSpec((None, None, n_ctx, LANES), lambda bi, hp, rb: (bi, l, 0, hp)),
                  pl.BlockSpec((NA_WIN_R, 2, GRID_W, n_loc), lambda bi, hp, rb: (0, hp, 0, 0))],
        out_specs=pl.BlockSpec((None, tq, LANES), lambda bi, hp, rb: (bi, rb, hp)),
        out_shape=jax.ShapeDtypeStruct((b, s, NA_HEADS * NA_HEAD_DIM), BF16),
        compiler_params=_params(("arbitrary", "arbitrary", "arbitrary")),
    )(p3, p3, p3, p3, ck4, cv4, bias)


def _dft_tables():
    gd = FNET_GROUP_DIM
    n = np.arange(gd)
    ang = 2.0 * np.pi * ((n[:, None] * n[None, :]) % gd) / gd
    fc = np.concatenate([np.cos(ang), -np.sin(ang)], axis=1) / 16.0
    fl = np.concatenate([np.cos(ang), np.sin(ang)], axis=1) / 16.0
    npos = GRID_H * GRID_W
    k1 = np.arange(64)[None, :, None]
    col = np.arange(64)[:, None, None]
    row = np.arange(64)[None, None, :]
    a1 = 2.0 * np.pi * ((k1 * (64 * row + col)) % npos) / npos
    t1 = np.concatenate([np.cos(a1), np.sin(a1)], axis=2) / 8.0
    k = np.arange(64)
    a2 = 2.0 * np.pi * ((k[:, None] * k[None, :]) % 64) / 64.0
    t2 = np.concatenate([np.cos(a2), np.sin(a2)], axis=1) / 8.0
    cast = lambda a: jnp.asarray(a.astype(np.float32)).astype(BF16)
    return cast(fc), cast(fl), cast(t1), cast(t2)


def _fft_prompt_kernel(h_ref, fg_ref, fc_ref, fl_ref, o_ref):
    gd = FNET_GROUP_DIM
    z = jnp.dot(h_ref[...].astype(BF16), fc_ref[...], preferred_element_type=F32)
    stack = jnp.concatenate([z[:, :gd], z[:, gd:]], axis=0).astype(BF16)
    y = jnp.dot(fl_ref[...], stack, preferred_element_type=F32)
    o_ref[...] = (y * _silu(fg_ref[...].astype(F32))).astype(BF16)


def _fourier_prompt(h2d, p2d, fc, fl, b, s):
    gd = FNET_GROUP_DIM
    cf = _OFF["f_gate"] // gd
    return pl.pallas_call(
        _fft_prompt_kernel,
        grid=(b, FNET_GROUPS),
        in_specs=[pl.BlockSpec((s, gd), lambda bi, g: (bi, g)),
                  pl.BlockSpec((s, gd), lambda bi, g: (bi, cf + g)),
                  pl.BlockSpec((gd, 2 * gd), lambda bi, g: (0, 0)),
                  pl.BlockSpec((s, 2 * s), lambda bi, g: (0, 0))],
        out_specs=pl.BlockSpec((s, gd), lambda bi, g: (bi, g)),
        out_shape=jax.ShapeDtypeStruct((b * s, D_MODEL), BF16),
        compiler_params=_params(("arbitrary", "arbitrary")),
    )(h2d, p2d, fc, fl)


def _fft_latent_kernel(h_ref, fg_ref, fc_ref, t1_ref, t2_ref, o_ref, z_ref, a_ref, y_ref):
    gd = FNET_GROUP_DIM
    n = GRID_H * GRID_W
    rows = 512
    nt = 2 * gd // LANES
    for c in range(n // rows):
        sl = slice(c * rows, (c + 1) * rows)
        z = jnp.dot(h_ref[sl, :].astype(BF16), fc_ref[...], preferred_element_type=F32)
        for t in range(nt):
            z_ref[t, sl, :] = z[:, t * LANES:(t + 1) * LANES]

    def stage1(col, carry):
        tiles = [z_ref[t, pl.ds(col, GRID_H, stride=GRID_W), :] for t in range(nt)]
        zr = jnp.concatenate(tiles[:nt // 2], axis=1)
        zi = jnp.concatenate(tiles[nt // 2:], axis=1)
        stack = jnp.concatenate([jnp.concatenate([zr, zi], axis=1),
                                 jnp.concatenate([zi, -zr], axis=1)], axis=0).astype(BF16)
        a = jnp.dot(t1_ref[col], stack, preferred_element_type=F32)
        off = pl.multiple_of(col * GRID_H, GRID_H)
        for t in range(nt):
            a_ref[t, pl.ds(off, GRID_H), :] = a[:, t * LANES:(t + 1) * LANES]
        return carry

    lax.fori_loop(0, GRID_W, stage1, 0)

    def stage2(k1, carry):
        tiles = [a_ref[t, pl.ds(k1, GRID_W, stride=GRID_H), :] for t in range(nt)]
        stack = jnp.concatenate([jnp.concatenate(tiles[:nt // 2], axis=1),
                                 jnp.concatenate(tiles[nt // 2:], axis=1)], axis=0).astype(BF16)
        y = jnp.dot(t2_ref[...], stack, preferred_element_type=F32)
        for t in range(nt // 2):
            y_ref[t, pl.ds(k1, GRID_H, stride=GRID_W), :] = y[:, t * LANES:(t + 1) * LANES]
        return carry

    lax.fori_loop(0, GRID_H, stage2, 0)
    for c in range(n // rows):
        sl = slice(c * rows, (c + 1) * rows)
        y = jnp.concatenate([y_ref[t, sl, :] for t in range(nt // 2)], axis=1)
        o_ref[sl, :] = (y * _silu(fg_ref[sl, :].astype(F32))).astype(BF16)


def _fourier_latent(h2d, p2d, fc, t1, t2, b):
    gd = FNET_GROUP_DIM
    n = GRID_H * GRID_W
    cf = _OFF["f_gate"] // gd
    return pl.pallas_call(
        _fft_latent_kernel,
        grid=(b, FNET_GROUPS),
        in_specs=[pl.BlockSpec((n, gd), lambda bi, g: (bi, g)),
                  pl.BlockSpec((n, gd), lambda bi, g: (bi, cf + g)),
                  pl.BlockSpec((gd, 2 * gd), lambda bi, g: (0, 0)),
                  pl.BlockSpec((GRID_W, GRID_H, 2 * GRID_H), lambda bi, g: (0, 0, 0)),
                  pl.BlockSpec((GRID_H, 2 * GRID_W), lambda bi, g: (0, 0))],
        out_specs=pl.BlockSpec((n, gd), lambda bi, g: (bi, g)),
        out_shape=jax.ShapeDtypeStruct((b * n, D_MODEL), BF16),
        scratch_shapes=[pltpu.VMEM((2 * gd // LANES, n, LANES), F32), pltpu.VMEM((2 * gd // LANES, n, LANES), F32),
                        pltpu.VMEM((gd // LANES, n, LANES), F32)],
        compiler_params=_params(("arbitrary", "arbitrary")),
    )(h2d, p2d, fc, t1, t2)


def _merge_kernel(x_ref, mod_ref, mg_ref, u_ref, v_ref, sgate_ref, yd_ref, yn_ref, yf_ref,
                  gv_ref, sw_ref, sb_ref, psg_ref, pd_ref, pn_ref, pf_ref, wo_ref, gpost_ref,
                  o_ref, ysg_ref, *, latent, tm, tiles_per_batch):
    i = pl.program_id(0)
    lane = lax.broadcasted_iota(jnp.int32, (CHUNK, LANES), 1)
    for c in range(tm // CHUNK):
        sl = slice(c * CHUNK, (c + 1) * CHUNK)
        v = v_ref[sl, :].astype(F32)
        vn = (v * lax.rsqrt(jnp.mean(v * v, axis=-1, keepdims=True) + EPS) * gv_ref[...]).astype(BF16)
        parts = []
        for pr in range(SG_GROUPS // 2):
            vp = vn[:, pr * LANES:(pr + 1) * LANES]
            s0 = jnp.dot(sw_ref[2 * pr], vp, preferred_element_type=F32)
            s1 = jnp.dot(sw_ref[2 * pr + 1], vp, preferred_element_type=F32)
            parts.append(jnp.where(lane < SG_WIDTH // SG_GROUPS, s0, s1))
        s = jnp.concatenate(parts, axis=1) + sb_ref[...]
        y = u_ref[sl, :].astype(F32) * s * _silu(sgate_ref[sl, :].astype(F32))
        ysg_ref[sl, :] = y.astype(BF16)

    def gate(n):
        return jax.nn.sigmoid(mg_ref[:, n * D_MODEL:(n + 1) * D_MODEL].astype(F32))

    mixed = gate(0) * jnp.dot(ysg_ref[...], psg_ref[...], preferred_element_type=F32)
    mixed = mixed + gate(1) * jnp.dot(yd_ref[...], pd_ref[...], preferred_element_type=F32)
    mixed = mixed + gate(2) * jnp.dot(yn_ref[...], pn_ref[...], preferred_element_type=F32)
    mixed = mixed + gate(3) * jnp.dot(yf_ref[...], pf_ref[...], preferred_element_type=F32)
    out = jnp.dot(mixed.astype(BF16), wo_ref[...], preferred_element_type=F32)
    out = out * lax.rsqrt(jnp.mean(out * out, axis=-1, keepdims=True) + EPS) * gpost_ref[...]
    row = (1 + i // tiles_per_batch) if latent else 0
    g = mod_ref[pl.ds(row, 1), :][:, 2 * D_MODEL:]
    o_ref[...] = x_ref[...] + g * out


def _merge(x2d, mod_all, p2d, yd, yn, yf, w, l, *, latent, seq, tm=256):
    t = x2d.shape[0]
    cu, cv, cs = (_OFF[n] // SG_WIDTH for n in ("sg_u", "sg_v", "sg_gate"))
    tok = lambda width, c=0: pl.BlockSpec((tm, width), lambda i: (i, c))
    wspec = lambda *shape: pl.BlockSpec((None,) + shape, lambda i: (l,) + (0,) * len(shape))
    tiles_per_batch = seq // tm if latent else 1
    return pl.pallas_call(
        functools.partial(_merge_kernel, latent=latent, tm=tm, tiles_per_batch=tiles_per_batch),
        grid=(t // tm,),
        in_specs=[tok(D_MODEL), wspec(8, 3 * D_MODEL), tok(N_BRANCH * D_MODEL, 0),
                  tok(SG_WIDTH, cu), tok(SG_WIDTH, cv), tok(SG_WIDTH, cs),
                  tok(512), tok(512), tok(D_MODEL),
                  wspec(1, SG_WIDTH), wspec(SG_GROUPS, CHUNK, CHUNK), wspec(CHUNK, SG_WIDTH),
                  wspec(SG_WIDTH, D_MODEL), wspec(512, D_MODEL), wspec(512, D_MODEL),
                  wspec(D_MODEL, D_MODEL), wspec(D_MODEL, D_MODEL), wspec(1, D_MODEL)],
        out_specs=tok(D_MODEL),
        out_shape=jax.ShapeDtypeStruct((t, D_MODEL), F32),
        scratch_shapes=[pltpu.VMEM((tm, SG_WIDTH), BF16)],
        compiler_params=_params(("arbitrary",)),
    )(x2d, mod_all, p2d, p2d, p2d, p2d, yd, yn, yf,
      w["sg_norm_g"], w["sg_w"], w["sg_b"], w["p_sg"], w["p_diff"], w["p_na"], w["p_fnet"], w["w_out"],
      w["g_post"])


def _rope_tables(n_tokens):
    nf = DIFF_HEAD_DIM // 4
    t = np.arange(n_tokens)
    inv = ROPE_BASE ** (-np.arange(nf, dtype=np.float64) / nf)
    ang_r = (t // GRID_W).astype(np.float64)[:, None] * inv[None, :]
    ang_c = (t % GRID_W).astype(np.float64)[:, None] * inv[None, :]
    z = np.zeros_like(ang_r)
    cos64 = np.concatenate([np.cos(ang_r), np.cos(ang_r), np.cos(ang_c), np.cos(ang_c)], axis=1)
    sina64 = np.concatenate([-np.sin(ang_r), z, -np.sin(ang_c), z], axis=1)
    sinb64 = np.concatenate([z, np.sin(ang_r), z, np.sin(ang_c)], axis=1)
    two = lambda a: jnp.asarray(np.concatenate([a, a], axis=1).astype(np.float32))
    return two(cos64), two(sina64), two(sinb64)


def kernel(x_prompt, x_sample, cache_diff_k, cache_diff_v, cache_na_k, cache_na_v, c, c_ctx,
           w_mod, b_mod, g_pre, g_post, w_in, sg_norm_g, sg_w, sg_b,
           diff_lam_q1, diff_lam_k1, diff_lam_q2, diff_lam_k2, diff_subln_g, na_rpb,
           w_proj_sg, w_proj_diff, w_proj_na, w_proj_fnet, w_out):
    bp, sp, _ = x_prompt.shape
    bs, ss, _ = x_sample.shape
    n_ctx = cache_diff_k.shape[2]

    cond8 = jnp.zeros((8, D_MODEL), F32).at[0].set(c_ctx).at[1:1 + bs].set(c)
    mod_all = _modulation(cond8, w_mod, b_mod)

    w_in_b = jnp.concatenate(
        [w_in[:, :, _REF_OFF[n][0]:_REF_OFF[n][0] + _REF_OFF[n][1]] for n in _MY_ORDER], axis=-1).astype(BF16)
    weights = {
        "sg_norm_g": sg_norm_g.reshape(DEPTH, 1, SG_WIDTH),
        "sg_w": sg_w.astype(BF16),
        "sg_b": jnp.repeat(jnp.transpose(sg_b, (0, 2, 1)), SG_WIDTH // SG_GROUPS, axis=2),
        "p_sg": w_proj_sg.astype(BF16), "p_diff": w_proj_diff.astype(BF16),
        "p_na": w_proj_na.astype(BF16), "p_fnet": w_proj_fnet.astype(BF16),
        "w_out": w_out.astype(BF16), "g_post": g_post.reshape(DEPTH, 1, D_MODEL),
    }
    g_pre3 = g_pre.reshape(DEPTH, 1, D_MODEL)
    lamp = jnp.stack([diff_lam_q1, diff_lam_k1, diff_lam_q2, diff_lam_k2], axis=1)
    subg3 = diff_subln_g.reshape(DEPTH, 1, 2 * DIFF_HEAD_DIM)
    rope_tabs = _rope_tables(ss)
    fc, fl, t1, t2 = _dft_tables()
    ckd = cache_diff_k.reshape(bs, DEPTH, n_ctx, DIFF_HEADS * 2 * DIFF_HEAD_DIM)
    cvd = cache_diff_v.reshape(bs, DEPTH, n_ctx, DIFF_HEADS * 2 * DIFF_HEAD_DIM)
    ckn = cache_na_k.reshape(bs, DEPTH, n_ctx, NA_HEADS * NA_HEAD_DIM)
    cvn = cache_na_v.reshape(bs, DEPTH, n_ctx, NA_HEADS * NA_HEAD_DIM)

    xp = x_prompt.reshape(bp * sp, D_MODEL)
    xs = x_sample.reshape(bs * ss, D_MODEL)
    caches = [[], [], [], []]
    for l in range(DEPTH):
        pp, hp, dk32, dv32, nk32, nv32 = _inproj(xp, mod_all, g_pre3, w_in_b, l, latent=False)
        for lst, arr in zip(caches, (dk32, dv32, nk32, nv32)):
            lst.append(arr)
        pp3 = pp.reshape(bp, sp, IN_WIDTH)
        yd = _diff_attention(pp3, lamp, subg3, l, tq=sp).reshape(bp * sp, -1)
        yn = _context_attention(pp3).reshape(bp * sp, -1)
        yf = _fourier_prompt(hp, pp, fc, fl, bp, sp)
        xp = _merge(xp, mod_all, pp, yd, yn, yf, weights, l, latent=False, seq=sp)
        ps, hs = _inproj(xs, mod_all, g_pre3, w_in_b, l, latent=True, rope_tabs=rope_tabs, seq=ss)
        ps3 = ps.reshape(bs, ss, IN_WIDTH)
        yd = _diff_attention(ps3, lamp, subg3, l, tq=256, ctx=(ckd, cvd)).reshape(bs * ss, -1)
        bias = _na_bias_table(na_rpb[l])
        yn = _neighbourhood_attention(ps3, ckn, cvn, bias, l).reshape(bs * ss, -1)
        yf = _fourier_latent(hs, ps, fc, t1, t2, bs)
        xs = _merge(xs, mod_all, ps, yd, yn, yf, weights, l, latent=True, seq=ss)

    def stack(lst, heads, hd):
        return jnp.stack([a.reshape(bp, sp, heads, hd) for a in lst], axis=1)

    return (xp.reshape(bp, sp, D_MODEL), xs.reshape(bs, ss, D_MODEL),
            stack(caches[0], DIFF_HEADS, 2 * DIFF_HEAD_DIM), stack(caches[1], DIFF_HEADS, 2 * DIFF_HEAD_DIM),
            stack(caches[2], NA_HEADS, NA_HEAD_DIM), stack(caches[3], NA_HEADS, NA_HEAD_DIM))
```

```python
import functools
import math

import numpy as np
import jax
import jax.numpy as jnp
from jax import lax
from jax.experimental import pallas as pl
from jax.experimental.pallas import tpu as pltpu

F32 = jnp.float32
BF16 = jnp.bfloat16

D_MODEL = 1024
DEPTH = 4
GRID_W = 64
GRID_H = 64
EPS = 1e-6
ROPE_BASE = 10000.0
CHUNK = 128
SG_WIDTH = 512
SG_GROUPS = 8
DIFF_HEADS = 4
DIFF_HEAD_DIM = 64
NA_HEADS = 8
NA_HEAD_DIM = 64
NA_WIN_R = 8
NA_WIN_C = 16
FNET_GROUPS = 4
FNET_GROUP_DIM = 256
N_BRANCH = 4
LANES = 128
VMEM_LIMIT = 56 * 1024 * 1024

_REF_SEGMENTS = (("sg_u", 512), ("sg_v", 512), ("sg_gate", 512),
                 ("dq", 512), ("dk", 512), ("dv", 512), ("d_gate", 512),
                 ("nq", 512), ("nk", 512), ("nv", 512), ("n_gate", 512),
                 ("f_gate", 1024), ("merge", 4096))
_MY_ORDER = ("merge", "f_gate", "sg_u", "sg_v", "sg_gate", "dq", "dk", "dv", "d_gate",
             "nq", "nk", "nv", "n_gate")
UNIT = 512


def _layout():
    ref_off, o = {}, 0
    for name, width in _REF_SEGMENTS:
        ref_off[name] = (o, width)
        o += width
    my_off, o, unit_names = {}, 0, []
    for name in _MY_ORDER:
        my_off[name] = o
        o += ref_off[name][1]
        unit_names += [name] * (ref_off[name][1] // UNIT)
    return ref_off, my_off, o, tuple(unit_names)


_REF_OFF, _OFF, IN_WIDTH, _UNIT_NAMES = _layout()
N_COL_TILES = 3
TN = IN_WIDTH // N_COL_TILES
UNITS_PER_TILE = TN // UNIT


def _silu(x):
    return x * jax.nn.sigmoid(x)


def _params(sem):
    return pltpu.CompilerParams(dimension_semantics=sem, vmem_limit_bytes=VMEM_LIMIT)


def _mod_kernel(cond_ref, w_ref, b_ref, o_ref):
    s = _silu(cond_ref[...])
    o_ref[...] = jnp.dot(s, w_ref[...], precision=lax.Precision.HIGHEST,
                         preferred_element_type=F32) + b_ref[...]


def _modulation(cond8, w_mod, b_mod):
    b3 = b_mod.reshape(DEPTH, 1, 3 * D_MODEL)
    return pl.pallas_call(
        _mod_kernel,
        grid=(DEPTH, 3),
        in_specs=[pl.BlockSpec((8, D_MODEL), lambda l, j: (0, 0)),
                  pl.BlockSpec((None, D_MODEL, D_MODEL), lambda l, j: (l, 0, j)),
                  pl.BlockSpec((None, 1, D_MODEL), lambda l, j: (l, 0, j))],
        out_specs=pl.BlockSpec((None, 8, D_MODEL), lambda l, j: (l, 0, j)),
        out_shape=jax.ShapeDtypeStruct((DEPTH, 8, 3 * D_MODEL), F32),
        compiler_params=_params(("arbitrary", "arbitrary")),
        name="modulation",
    )(cond8, w_mod, b3)


def _rope(acc, cos_ref, sina_ref, sinb_ref):
    cos, sina, sinb = cos_ref[...], sina_ref[...], sinb_ref[...]
    outs = []
    for hd in range(acc.shape[1] // LANES):
        a = acc[:, hd * LANES:(hd + 1) * LANES]
        up = pltpu.roll(a, LANES - 16, 1)
        dn = pltpu.roll(a, 16, 1)
        outs.append(a * cos + up * sina + dn * sinb)
    return jnp.concatenate(outs, axis=1)


def _inproj_kernel(*refs, latent, tm, seq, tiles_per_batch, n_alias):
    if latent:
        (x_ref, mod_ref, gpre_ref, w_ref, cos_ref, sina_ref, sinb_ref, p_ref, h_ref) = refs
        cache_refs = {}
    else:
        (x_ref, mod_ref, gpre_ref, w_ref) = refs[:4]
        (p_ref, h_ref, dk_ref, dv_ref, nk_ref, nv_ref) = refs[4 + n_alias:]
        cache_refs = {"dk": dk_ref, "dv": dv_ref, "nk": nk_ref, "nv": nv_ref}
    i = pl.program_id(0)
    j = pl.program_id(1)

    @pl.when(j == 0)
    def _():
        x = x_ref[...]
        r = lax.rsqrt(jnp.mean(x * x, axis=-1, keepdims=True) + EPS)
        row = (1 + i // tiles_per_batch) if latent else 0
        m = mod_ref[pl.ds(row, 1), :]
        shift = m[:, :D_MODEL]
        scale = m[:, D_MODEL:2 * D_MODEL]
        h_ref[...] = ((x * r * gpre_ref[...]) * (1.0 + scale) + shift).astype(BF16)

    def unit(u, name):
        cols = slice(u * UNIT, (u + 1) * UNIT)
        acc = jnp.dot(h_ref[...], w_ref[:, cols], preferred_element_type=F32)
        if name in cache_refs:
            cache_refs[name][...] = acc.reshape(tm // seq, seq, UNIT)
        if name in ("dq", "nq"):
            acc = acc * (DIFF_HEAD_DIM ** -0.5)
        if latent and name in ("dq", "dk"):
            acc = _rope(acc, cos_ref, sina_ref, sinb_ref)
        p_ref[:, cols] = acc.astype(BF16)

    for jt in range(N_COL_TILES):
        @pl.when(j == jt)
        def _(jt=jt):
            for u in range(UNITS_PER_TILE):
                unit(u, _UNIT_NAMES[jt * UNITS_PER_TILE + u])


def _inproj(x2d, mod_all, g_pre3, w_in_b, l, *, latent, tm, seq, rope_tabs=None, cache_bufs=None):
    t = x2d.shape[0]
    n_i = t // tm
    in_specs = [pl.BlockSpec((tm, D_MODEL), lambda i, j: (i, 0)),
                pl.BlockSpec((None, 8, 3 * D_MODEL), lambda i, j: (l, 0, 0)),
                pl.BlockSpec((None, 1, D_MODEL), lambda i, j: (l, 0, 0)),
                pl.BlockSpec((None, D_MODEL, TN), lambda i, j: (l, 0, j))]
    args = [x2d, mod_all, g_pre3, w_in_b]
    out_specs = [pl.BlockSpec((tm, TN), lambda i, j: (i, j)),
                 pl.BlockSpec((tm, D_MODEL), lambda i, j: (i, 0))]
    out_shape = [jax.ShapeDtypeStruct((t, IN_WIDTH), BF16),
                 jax.ShapeDtypeStruct((t, D_MODEL), BF16)]
    tiles_per_batch = 1
    aliases = {}
    n_alias = 0
    if latent:
        tiles_per_batch = seq // tm
        for tab in rope_tabs:
            in_specs.append(pl.BlockSpec((tm, LANES), lambda i, j: (i % tiles_per_batch, 0)))
            args.append(tab)
    else:
        nb = tm // seq
        if cache_bufs is not None:
            n_alias = len(cache_bufs)
            for k, buf in enumerate(cache_bufs):
                in_specs.append(pl.BlockSpec(memory_space=pl.ANY))
                args.append(buf)
                aliases[4 + k] = 2 + k
        for _ in range(4):
            out_specs.append(pl.BlockSpec((nb, None, seq, UNIT), lambda i, j: (i, l, 0, 0)))
            out_shape.append(jax.ShapeDtypeStruct((t // seq, DEPTH, seq, UNIT), F32))
    return pl.pallas_call(
        functools.partial(_inproj_kernel, latent=latent, tm=tm, seq=seq, tiles_per_batch=tiles_per_batch,
                          n_alias=n_alias),
        grid=(n_i, N_COL_TILES),
        in_specs=in_specs,
        out_specs=out_specs,
        out_shape=out_shape,
        input_output_aliases=aliases,
        compiler_params=_params(("arbitrary", "arbitrary")),
        name="inproj_latent" if latent else "inproj_prompt",
    )(*args)


def _softmax_rows(s):
    m = jnp.max(s, axis=-1, keepdims=True)
    e = jnp.exp(s - m)
    return e * (1.0 / jnp.sum(e, axis=-1, keepdims=True))


def _diff_attn_kernel(*refs, lam_init, n_self, has_ctx):
    if has_ctx:
        (lamp_ref, subg_ref, q_ref, k_ref, v_ref, gate_ref, ck_ref, cv_ref, o_ref, kc_ref, vc_ref) = refs

        @pl.when(pl.program_id(2) == 0)
        def _():
            kc_ref[0:n_self, :] = k_ref[...]
            vc_ref[0:n_self, :] = v_ref[...]
            kc_ref[n_self:, :] = ck_ref[...].astype(BF16)
            vc_ref[n_self:, :] = cv_ref[...].astype(BF16)

        k = kc_ref[...]
        v = vc_ref[...]
    else:
        (lamp_ref, subg_ref, q_ref, k_ref, v_ref, gate_ref, o_ref) = refs
        k = k_ref[...]
        v = v_ref[...]
    q = q_ref[...]
    lane = lax.broadcasted_iota(jnp.int32, q.shape, 1)
    zero = jnp.zeros_like(q)
    q1 = jnp.where(lane < DIFF_HEAD_DIM, q, zero)
    q2 = jnp.where(lane >= DIFF_HEAD_DIM, q, zero)
    dn = (((1,), (1,)), ((), ()))
    p1 = _softmax_rows(lax.dot_general(q1, k, dn, preferred_element_type=F32))
    p2 = _softmax_rows(lax.dot_general(q2, k, dn, preferred_element_type=F32))
    lp = lamp_ref[...]
    lam = (jnp.exp(jnp.sum(lp[0:1] * lp[1:2], axis=-1, keepdims=True))
           - jnp.exp(jnp.sum(lp[2:3] * lp[3:4], axis=-1, keepdims=True)) + lam_init)
    a = (p1 - lam * p2).astype(BF16)
    o = jnp.dot(a, v, preferred_element_type=F32)
    y = o * lax.rsqrt(jnp.mean(o * o, axis=-1, keepdims=True) + EPS) * subg_ref[...]
    y = y * (1.0 - lam_init)
    o_ref[...] = (y * _silu(gate_ref[...].astype(F32))).astype(BF16)


def _diff_attention(p3, lamp, subg3, l, *, tq, ctx=None):
    b, s, _ = p3.shape
    cq, ck, cv, cg = (_OFF[n] // LANES for n in ("dq", "dk", "dv", "d_gate"))
    lam_init = 0.8 - 0.6 * math.exp(-0.3 * l)
    in_specs = [pl.BlockSpec((None, 4, DIFF_HEAD_DIM), lambda bi, h, qi: (l, 0, 0)),
                pl.BlockSpec((None, 1, LANES), lambda bi, h, qi: (l, 0, 0)),
                pl.BlockSpec((None, tq, LANES), lambda bi, h, qi: (bi, qi, cq + h)),
                pl.BlockSpec((None, s, LANES), lambda bi, h, qi: (bi, 0, ck + h)),
                pl.BlockSpec((None, s, LANES), lambda bi, h, qi: (bi, 0, cv + h)),
                pl.BlockSpec((None, tq, LANES), lambda bi, h, qi: (bi, qi, cg + h))]
    args = [lamp, subg3, p3, p3, p3, p3]
    scratch = []
    if ctx is not None:
        ck4, cv4 = ctx
        n_ctx = ck4.shape[2]
        for arr in (ck4, cv4):
            in_specs.append(pl.BlockSpec((None, None, n_ctx, LANES), lambda bi, h, qi: (bi, l, 0, h)))
            args.append(arr)
        scratch = [pltpu.VMEM((s + n_ctx, LANES), BF16), pltpu.VMEM((s + n_ctx, LANES), BF16)]
    return pl.pallas_call(
        functools.partial(_diff_attn_kernel, lam_init=lam_init, n_self=s, has_ctx=ctx is not None),
        grid=(b, DIFF_HEADS, s // tq),
        in_specs=in_specs,
        out_specs=pl.BlockSpec((None, tq, LANES), lambda bi, h, qi: (bi, qi, h)),
        out_shape=jax.ShapeDtypeStruct((b, s, DIFF_HEADS * LANES), BF16),
        scratch_shapes=scratch,
        compiler_params=_params(("arbitrary", "arbitrary", "arbitrary")),
        name="diff_attn_latent" if ctx is not None else "diff_attn_prompt",
    )(*args)


def _ctx_attn_kernel(q_ref, k_ref, v_ref, gate_ref, o_ref):
    q, k, v = q_ref[...], k_ref[...], v_ref[...]
    lane = lax.broadcasted_iota(jnp.int32, q.shape, 1)
    zero = jnp.zeros_like(q)
    dn = (((1,), (1,)), ((), ()))
    outs = []
    for hh in range(2):
        sel = (lane < NA_HEAD_DIM) if hh == 0 else (lane >= NA_HEAD_DIM)
        qh = jnp.where(sel, q, zero)
        p = _softmax_rows(lax.dot_general(qh, k, dn, preferred_element_type=F32)).astype(BF16)
        outs.append(jnp.dot(p, v, preferred_element_type=F32))
    o = jnp.where(lane < NA_HEAD_DIM, outs[0], outs[1])
    o_ref[...] = (o * _silu(gate_ref[...].astype(F32))).astype(BF16)


def _context_attention(p3):
    b, s, _ = p3.shape
    cq, ck, cv, cg = (_OFF[n] // LANES for n in ("nq", "nk", "nv", "n_gate"))
    spec = lambda c: pl.BlockSpec((None, s, LANES), lambda bi, hp: (bi, 0, c + hp))
    return pl.pallas_call(
        _ctx_attn_kernel,
        grid=(b, NA_HEADS // 2),
        in_specs=[spec(cq), spec(ck), spec(cv), spec(cg)],
        out_specs=pl.BlockSpec((None, s, LANES), lambda bi, hp: (bi, 0, hp)),
        out_shape=jax.ShapeDtypeStruct((b, s, NA_HEADS * NA_HEAD_DIM), BF16),
        compiler_params=_params(("arbitrary", "arbitrary")),
        name="ctx_attn_prompt",
    )(p3, p3, p3, p3)


def _na_attn_kernel(q_ref, k_ref, v_ref, gate_ref, ck_ref, cv_ref, bias_ref, o_ref, *, rows_per_step):
    rb = pl.program_id(2)
    ckb = ck_ref[...].astype(BF16)
    cvb = cv_ref[...].astype(BF16)
    n_loc = NA_WIN_R * GRID_W
    lane = lax.broadcasted_iota(jnp.int32, (GRID_W, LANES), 1)
    dn = (((1,), (1,)), ((), ()))

    def row_body(rl, carry):
        r = rb * rows_per_step + rl
        r0 = jnp.clip(r - NA_WIN_R // 2, 0, GRID_H - NA_WIN_R)
        start = r0 - r + (NA_WIN_R - 1)
        qoff = pl.multiple_of(rl * GRID_W, GRID_W)
        koff = pl.multiple_of(r0 * GRID_W, GRID_W)
        q = q_ref[pl.ds(qoff, GRID_W), :]
        kl = k_ref[pl.ds(koff, n_loc), :]
        vl = v_ref[pl.ds(koff, n_loc), :]
        zero = jnp.zeros_like(q)
        outs = []
        for hh in range(2):
            sel = (lane < NA_HEAD_DIM) if hh == 0 else (lane >= NA_HEAD_DIM)
            qh = jnp.where(sel, q, zero)
            s_loc = lax.dot_general(qh, kl, dn, preferred_element_type=F32) + bias_ref[start, hh]
            s_ctx = lax.dot_general(qh, ckb, dn, preferred_element_type=F32)
            m = jnp.maximum(jnp.max(s_loc, axis=-1, keepdims=True), jnp.max(s_ctx, axis=-1, keepdims=True))
            e_loc = jnp.exp(s_loc - m)
            e_ctx = jnp.exp(s_ctx - m)
            inv = 1.0 / (jnp.sum(e_loc, axis=-1, keepdims=True) + jnp.sum(e_ctx, axis=-1, keepdims=True))
            outs.append(jnp.dot((e_loc * inv).astype(BF16), vl, preferred_element_type=F32)
                        + jnp.dot((e_ctx * inv).astype(BF16), cvb, preferred_element_type=F32))
        o = jnp.where(lane < NA_HEAD_DIM, outs[0], outs[1])
        g = gate_ref[pl.ds(qoff, GRID_W), :].astype(F32)
        o_ref[pl.ds(qoff, GRID_W), :] = (o * _silu(g)).astype(BF16)
        return carry

    lax.fori_loop(0, rows_per_step, row_body, 0)


def _na_bias_tables(rpb):
    c = np.arange(GRID_W)
    c0 = np.clip(c - NA_WIN_C // 2, 0, GRID_W - NA_WIN_C)
    in_win = (c[None, :] >= c0[:, None]) & (c[None, :] < c0[:, None] + NA_WIN_C)
    period = 2 * GRID_W - 1
    lead = rpb.shape[:-1]
    w = jnp.concatenate([rpb[..., NA_WIN_C - 1:], jnp.zeros(lead + (period - (2 * NA_WIN_C - 1),), rpb.dtype),
                         rpb[..., :NA_WIN_C - 1]], axis=-1)
    flat = jnp.tile(w, (1,) * len(lead) + (GRID_W,))[..., :GRID_W * (period - 1)]
    toe = flat.reshape(lead + (GRID_W, period - 1))[..., :GRID_W]
    toe = jnp.where(in_win, toe.astype(F32), -1e30)
    tabs = jnp.stack([toe[:, :, s:s + NA_WIN_R] for s in range(NA_WIN_R)], axis=1)
    tabs = jnp.transpose(tabs, (0, 1, 2, 4, 3, 5))
    return tabs.reshape(DEPTH, NA_WIN_R, NA_HEADS, GRID_W, NA_WIN_R * GRID_W)


def _neighbourhood_attention(p3, ck4, cv4, bias, l, *, rows_per_step=8):
    b, s, _ = p3.shape
    cq, ck, cv, cg = (_OFF[n] // LANES for n in ("nq", "nk", "nv", "n_gate"))
    n_ctx = ck4.shape[2]
    tq = rows_per_step * GRID_W
    n_loc = NA_WIN_R * GRID_W
    return pl.pallas_call(
        functools.partial(_na_attn_kernel, rows_per_step=rows_per_step),
        grid=(b, NA_HEADS // 2, GRID_H // rows_per_step),
        in_specs=[pl.BlockSpec((None, tq, LANES), lambda bi, hp, rb: (bi, rb, cq + hp)),
                  pl.BlockSpec((None, s, LANES), lambda bi, hp, rb: (bi, 0, ck + hp)),
                  pl.BlockSpec((None, s, LANES), lambda bi, hp, rb: (bi, 0, cv + hp)),
                  pl.BlockSpec((None, tq, LANES), lambda bi, hp, rb: (bi, rb, cg + hp)),
                  pl.BlockSpec((None, None, n_ctx, LANES), lambda bi, hp, rb: (bi, l, 0, hp)),
                  pl.BlockSpec((None, None, n_ctx, LANES), lambda bi, hp, rb: (bi, l, 0, hp)),
                  pl.BlockSpec((None, NA_WIN_R, 2, GRID_W, n_loc), lambda bi, hp, rb: (l, 0, hp, 0, 0))],
        out_specs=pl.BlockSpec((None, tq, LANES), lambda bi, hp, rb: (bi, rb, hp)),
        out_shape=jax.ShapeDtypeStruct((b, s, NA_HEADS * NA_HEAD_DIM), BF16),
        compiler_params=_params(("arbitrary", "arbitrary", "arbitrary")),
        name="na_attn_latent",
    )(p3, p3, p3, p3, ck4, cv4, bias)


def _dft_tables():
    gd = FNET_GROUP_DIM
    n = np.arange(gd)
    ang = 2.0 * np.pi * ((n[:, None] * n[None, :]) % gd) / gd
    fc = np.concatenate([np.cos(ang), -np.sin(ang)], axis=1) / 16.0
    fl = np.concatenate([np.cos(ang), np.sin(ang)], axis=1) / 16.0
    npos = GRID_H * GRID_W
    k1 = np.arange(64)[None, :, None]
    col = np.arange(64)[:, None, None]
    row = np.arange(64)[None, None, :]
    a1 = 2.0 * np.pi * ((k1 * (64 * row + col)) % npos) / npos
    t1 = np.concatenate([np.cos(a1), np.sin(a1)], axis=2) / 8.0
    k = np.arange(64)
    a2 = 2.0 * np.pi * ((k[:, None] * k[None, :]) % 64) / 64.0
    t2 = np.concatenate([np.cos(a2), np.sin(a2)], axis=1) / 8.0
    cast = lambda a: jnp.asarray(a.astype(np.float32)).astype(BF16)
    return cast(fc), cast(fl), cast(t1), cast(t2)


def _fft_prompt_kernel(h_ref, fg_ref, fc_ref, fl_ref, o_ref):
    gd = FNET_GROUP_DIM
    z = jnp.dot(h_ref[...], fc_ref[...], preferred_element_type=F32)
    stack = jnp.concatenate([z[:, :gd], z[:, gd:]], axis=0).astype(BF16)
    y = jnp.dot(fl_ref[...], stack, preferred_element_type=F32)
    o_ref[...] = (y * _silu(fg_ref[...].astype(F32))).astype(BF16)


def _fourier_prompt(h2d, p2d, fc, fl, b, s):
    gd = FNET_GROUP_DIM
    cf = _OFF["f_gate"] // gd
    return pl.pallas_call(
        _fft_prompt_kernel,
        grid=(b, FNET_GROUPS),
        in_specs=[pl.BlockSpec((s, gd), lambda bi, g: (bi, g)),
                  pl.BlockSpec((s, gd), lambda bi, g: (bi, cf + g)),
                  pl.BlockSpec((gd, 2 * gd), lambda bi, g: (0, 0)),
                  pl.BlockSpec((s, 2 * s), lambda bi, g: (0, 0))],
        out_specs=pl.BlockSpec((s, gd), lambda bi, g: (bi, g)),
        out_shape=jax.ShapeDtypeStruct((b * s, D_MODEL), BF16),
        compiler_params=_params(("arbitrary", "arbitrary")),
        name="fourier_prompt",
    )(h2d, p2d, fc, fl)


def _fft_latent_kernel(h_ref, fg_ref, fc_ref, t1_ref, t2_ref, o_ref, z_ref, a_ref, y_ref):
    gd = FNET_GROUP_DIM
    n = GRID_H * GRID_W
    rows = 512
    nt = 2 * gd // LANES
    for c in range(n // rows):
        sl = slice(c * rows, (c + 1) * rows)
        z = jnp.dot(h_ref[sl, :], fc_ref[...], preferred_element_type=F32)
        for t in range(nt):
            z_ref[t, sl, :] = z[:, t * LANES:(t + 1) * LANES]

    def stage1(col, carry):
        tiles = [z_ref[t, pl.ds(col, GRID_H, stride=GRID_W), :] for t in range(nt)]
        zr = jnp.concatenate(tiles[:nt // 2], axis=1)
        zi = jnp.concatenate(tiles[nt // 2:], axis=1)
        stack = jnp.concatenate([jnp.concatenate([zr, zi], axis=1),
                                 jnp.concatenate([zi, -zr], axis=1)], axis=0).astype(BF16)
        a = jnp.dot(t1_ref[col], stack, preferred_element_type=F32)
        off = pl.multiple_of(col * GRID_H, GRID_H)
        for t in range(nt):
            a_ref[t, pl.ds(off, GRID_H), :] = a[:, t * LANES:(t + 1) * LANES]
        return carry

    lax.fori_loop(0, GRID_W, stage1, 0)

    def stage2(k1, carry):
        tiles = [a_ref[t, pl.ds(k1, GRID_W, stride=GRID_H), :] for t in range(nt)]
        stack = jnp.concatenate([jnp.concatenate(tiles[:nt // 2], axis=1),
                                 jnp.concatenate(tiles[nt // 2:], axis=1)], axis=0).astype(BF16)
        y = jnp.dot(t2_ref[...], stack, preferred_element_type=F32)
        for t in range(nt // 2):
            y_ref[t, pl.ds(k1, GRID_H, stride=GRID_W), :] = y[:, t * LANES:(t + 1) * LANES]
        return carry

    lax.fori_loop(0, GRID_H, stage2, 0)
    for c in range(n // rows):
        sl = slice(c * rows, (c + 1) * rows)
        y = jnp.concatenate([y_ref[t, sl, :] for t in range(nt // 2)], axis=1)
        o_ref[sl, :] = (y * _silu(fg_ref[sl, :].astype(F32))).astype(BF16)


def _fourier_latent(h2d, p2d, fc, t1, t2, b):
    gd = FNET_GROUP_DIM
    n = GRID_H * GRID_W
    cf = _OFF["f_gate"] // gd
    return pl.pallas_call(
        _fft_latent_kernel,
        grid=(b, FNET_GROUPS),
        in_specs=[pl.BlockSpec((n, gd), lambda bi, g: (bi, g)),
                  pl.BlockSpec((n, gd), lambda bi, g: (bi, cf + g)),
                  pl.BlockSpec((gd, 2 * gd), lambda bi, g: (0, 0)),
                  pl.BlockSpec((GRID_W, GRID_H, 2 * GRID_H), lambda bi, g: (0, 0, 0)),
                  pl.BlockSpec((GRID_H, 2 * GRID_W), lambda bi, g: (0, 0))],
        out_specs=pl.BlockSpec((n, gd), lambda bi, g: (bi, g)),
        out_shape=jax.ShapeDtypeStruct((b * n, D_MODEL), BF16),
        scratch_shapes=[pltpu.VMEM((2 * gd // LANES, n, LANES), F32), pltpu.VMEM((2 * gd // LANES, n, LANES), F32),
                        pltpu.VMEM((gd // LANES, n, LANES), F32)],
        compiler_params=_params(("arbitrary", "arbitrary")),
        name="fourier_latent",
    )(h2d, p2d, fc, t1, t2)


def _merge_kernel(x_ref, mod_ref, mg_ref, u_ref, v_ref, sgate_ref, yd_ref, yn_ref, yf_ref,
                  gv_ref, sw_ref, sb_ref, psg_ref, pd_ref, pn_ref, pf_ref, wo_ref, gpost_ref,
                  o_ref, ysg_ref, *, latent, tm, tiles_per_batch):
    i = pl.program_id(0)
    lane = lax.broadcasted_iota(jnp.int32, (CHUNK, LANES), 1)
    for c in range(tm // CHUNK):
        sl = slice(c * CHUNK, (c + 1) * CHUNK)
        v = v_ref[sl, :].astype(F32)
        vn = (v * lax.rsqrt(jnp.mean(v * v, axis=-1, keepdims=True) + EPS) * gv_ref[...]).astype(BF16)
        parts = []
        for pr in range(SG_GROUPS // 2):
            vp = vn[:, pr * LANES:(pr + 1) * LANES]
            s0 = jnp.dot(sw_ref[2 * pr], vp, preferred_element_type=F32)
            s1 = jnp.dot(sw_ref[2 * pr + 1], vp, preferred_element_type=F32)
            parts.append(jnp.where(lane < SG_WIDTH // SG_GROUPS, s0, s1))
        s = jnp.concatenate(parts, axis=1) + sb_ref[...]
        y = u_ref[sl, :].astype(F32) * s * _silu(sgate_ref[sl, :].astype(F32))
        ysg_ref[sl, :] = y.astype(BF16)

    def gate(n):
        return jax.nn.sigmoid(mg_ref[:, n * D_MODEL:(n + 1) * D_MODEL].astype(F32))

    mixed = gate(0) * jnp.dot(ysg_ref[...], psg_ref[...], preferred_element_type=F32)
    mixed = mixed + gate(1) * jnp.dot(yd_ref[...], pd_ref[...], preferred_element_type=F32)
    mixed = mixed + gate(2) * jnp.dot(yn_ref[...], pn_ref[...], preferred_element_type=F32)
    mixed = mixed + gate(3) * jnp.dot(yf_ref[...], pf_ref[...], preferred_element_type=F32)
    out = jnp.dot(mixed.astype(BF16), wo_ref[...], preferred_element_type=F32)
    out = out * lax.rsqrt(jnp.mean(out * out, axis=-1, keepdims=True) + EPS) * gpost_ref[...]
    row = (1 + i // tiles_per_batch) if latent else 0
    g = mod_ref[pl.ds(row, 1), :][:, 2 * D_MODEL:]
    o_ref[...] = x_ref[...] + g * out


def _merge(x2d, mod_all, p2d, yd, yn, yf, w, l, *, latent, seq, tm=256):
    t = x2d.shape[0]
    cu, cv, cs = (_OFF[n] // SG_WIDTH for n in ("sg_u", "sg_v", "sg_gate"))
    tok = lambda width, c=0: pl.BlockSpec((tm, width), lambda i: (i, c))
    wspec = lambda *shape: pl.BlockSpec((None,) + shape, lambda i: (l,) + (0,) * len(shape))
    tiles_per_batch = seq // tm if latent else 1
    return pl.pallas_call(
        functools.partial(_merge_kernel, latent=latent, tm=tm, tiles_per_batch=tiles_per_batch),
        grid=(t // tm,),
        in_specs=[tok(D_MODEL), wspec(8, 3 * D_MODEL), tok(N_BRANCH * D_MODEL, 0),
                  tok(SG_WIDTH, cu), tok(SG_WIDTH, cv), tok(SG_WIDTH, cs),
                  tok(512), tok(512), tok(D_MODEL),
                  wspec(1, SG_WIDTH), wspec(SG_GROUPS, CHUNK, CHUNK), wspec(CHUNK, SG_WIDTH),
                  wspec(SG_WIDTH, D_MODEL), wspec(512, D_MODEL), wspec(512, D_MODEL),
                  wspec(D_MODEL, D_MODEL), wspec(D_MODEL, D_MODEL), wspec(1, D_MODEL)],
        out_specs=tok(D_MODEL),
        out_shape=jax.ShapeDtypeStruct((t, D_MODEL), F32),
        scratch_shapes=[pltpu.VMEM((tm, SG_WIDTH), BF16)],
        compiler_params=_params(("arbitrary",)),
        name="merge_latent" if latent else "merge_prompt",
    )(x2d, mod_all, p2d, p2d, p2d, p2d, yd, yn, yf,
      w["sg_norm_g"], w["sg_w"], w["sg_b"], w["p_sg"], w["p_diff"], w["p_na"], w["p_fnet"], w["w_out"],
      w["g_post"])


def _rope_tables(n_tokens):
    nf = DIFF_HEAD_DIM // 4
    t = np.arange(n_tokens)
    inv = ROPE_BASE ** (-np.arange(nf, dtype=np.float64) / nf)
    ang_r = (t // GRID_W).astype(np.float64)[:, None] * inv[None, :]
    ang_c = (t % GRID_W).astype(np.float64)[:, None] * inv[None, :]
    z = np.zeros_like(ang_r)
    cos64 = np.concatenate([np.cos(ang_r), np.cos(ang_r), np.cos(ang_c), np.cos(ang_c)], axis=1)
    sina64 = np.concatenate([-np.sin(ang_r), z, -np.sin(ang_c), z], axis=1)
    sinb64 = np.concatenate([z, np.sin(ang_r), z, np.sin(ang_c)], axis=1)
    two = lambda a: jnp.asarray(np.concatenate([a, a], axis=1).astype(np.float32))
    return two(cos64), two(sina64), two(sinb64)


def kernel(x_prompt, x_sample, cache_diff_k, cache_diff_v, cache_na_k, cache_na_v, c, c_ctx,
           w_mod, b_mod, g_pre, g_post, w_in, sg_norm_g, sg_w, sg_b,
           diff_lam_q1, diff_lam_k1, diff_lam_q2, diff_lam_k2, diff_subln_g, na_rpb,
           w_proj_sg, w_proj_diff, w_proj_na, w_proj_fnet, w_out):
    bp, sp, _ = x_prompt.shape
    bs, ss, _ = x_sample.shape
    n_ctx = cache_diff_k.shape[2]

    cond8 = jnp.zeros((8, D_MODEL), F32).at[0].set(c_ctx).at[1:1 + bs].set(c)
    mod_all = _modulation(cond8, w_mod, b_mod)

    w_in_b = jnp.concatenate(
        [w_in[:, :, _REF_OFF[n][0]:_REF_OFF[n][0] + _REF_OFF[n][1]] for n in _MY_ORDER], axis=-1).astype(BF16)
    weights = {
        "sg_norm_g": sg_norm_g.reshape(DEPTH, 1, SG_WIDTH),
        "sg_w": sg_w.astype(BF16),
        "sg_b": jnp.repeat(jnp.transpose(sg_b, (0, 2, 1)), SG_WIDTH // SG_GROUPS, axis=2),
        "p_sg": w_proj_sg.astype(BF16), "p_diff": w_proj_diff.astype(BF16),
        "p_na": w_proj_na.astype(BF16), "p_fnet": w_proj_fnet.astype(BF16),
        "w_out": w_out.astype(BF16), "g_post": g_post.reshape(DEPTH, 1, D_MODEL),
    }
    g_pre3 = g_pre.reshape(DEPTH, 1, D_MODEL)
    lamp = jnp.stack([diff_lam_q1, diff_lam_k1, diff_lam_q2, diff_lam_k2], axis=1)
    subg3 = diff_subln_g.reshape(DEPTH, 1, 2 * DIFF_HEAD_DIM)
    rope_tabs = _rope_tables(ss)
    fc, fl, t1, t2 = _dft_tables()
    bias = _na_bias_tables(na_rpb)
    ckd = cache_diff_k.reshape(bs, DEPTH, n_ctx, DIFF_HEADS * 2 * DIFF_HEAD_DIM)
    cvd = cache_diff_v.reshape(bs, DEPTH, n_ctx, DIFF_HEADS * 2 * DIFF_HEAD_DIM)
    ckn = cache_na_k.reshape(bs, DEPTH, n_ctx, NA_HEADS * NA_HEAD_DIM)
    cvn = cache_na_v.reshape(bs, DEPTH, n_ctx, NA_HEADS * NA_HEAD_DIM)

    xp = x_prompt.reshape(bp * sp, D_MODEL)
    xs = x_sample.reshape(bs * ss, D_MODEL)
    caches = None
    for l in range(DEPTH):
        pp, hp, *caches = _inproj(xp, mod_all, g_pre3, w_in_b, l, latent=False, tm=2 * sp, seq=sp,
                                  cache_bufs=caches)
        pp3 = pp.reshape(bp, sp, IN_WIDTH)
        yd = _diff_attention(pp3, lamp, subg3, l, tq=sp).reshape(bp * sp, -1)
        yn = _context_attention(pp3).reshape(bp * sp, -1)
        yf = _fourier_prompt(hp, pp, fc, fl, bp, sp)
        xp = _merge(xp, mod_all, pp, yd, yn, yf, weights, l, latent=False, seq=sp)
        ps, hs = _inproj(xs, mod_all, g_pre3, w_in_b, l, latent=True, tm=1024, seq=ss, rope_tabs=rope_tabs)
        ps3 = ps.reshape(bs, ss, IN_WIDTH)
        yd = _diff_attention(ps3, lamp, subg3, l, tq=256, ctx=(ckd, cvd)).reshape(bs * ss, -1)
        yn = _neighbourhood_attention(ps3, ckn, cvn, bias, l).reshape(bs * ss, -1)
        yf = _fourier_latent(hs, ps, fc, t1, t2, bs)
        xs = _merge(xs, mod_all, ps, yd, yn, yf, weights, l, latent=True, seq=ss)

    dk5, dv5, nk5, nv5 = caches
    return (xp.reshape(bp, sp, D_MODEL), xs.reshape(bs, ss, D_MODEL),
            dk5.reshape(bp, DEPTH, sp, DIFF_HEADS, 2 * DIFF_HEAD_DIM),
            dv5.reshape(bp, DEPTH, sp, DIFF_HEADS, 2 * DIFF_HEAD_DIM),
            nk5.reshape(bp, DEPTH, sp, NA_HEADS, NA_HEAD_DIM),
            nv5.reshape(bp, DEPTH, sp, NA_HEADS, NA_HEAD_DIM))
```

```python
import functools
import math

import numpy as np
import jax
import jax.numpy as jnp
from jax import lax
from jax.experimental import pallas as pl
from jax.experimental.pallas import tpu as pltpu

F32 = jnp.float32
BF16 = jnp.bfloat16

D_MODEL = 1024
DEPTH = 4
GRID_W = 64
GRID_H = 64
EPS = 1e-6
ROPE_BASE = 10000.0
CHUNK = 128
SG_WIDTH = 512
SG_GROUPS = 8
DIFF_HEADS = 4
DIFF_HEAD_DIM = 64
NA_HEADS = 8
NA_HEAD_DIM = 64
NA_WIN_R = 8
NA_WIN_C = 16
FNET_GROUPS = 4
FNET_GROUP_DIM = 256
N_BRANCH = 4
LANES = 128
VMEM_LIMIT = 56 * 1024 * 1024

_REF_SEGMENTS = (("sg_u", 512), ("sg_v", 512), ("sg_gate", 512),
                 ("dq", 512), ("dk", 512), ("dv", 512), ("d_gate", 512),
                 ("nq", 512), ("nk", 512), ("nv", 512), ("n_gate", 512),
                 ("f_gate", 1024), ("merge", 4096))
_MY_ORDER = ("merge", "f_gate", "sg_u", "sg_v", "sg_gate", "dq", "dk", "dv", "d_gate",
             "nq", "nk", "nv", "n_gate")
UNIT = 512


def _layout():
    ref_off, o = {}, 0
    for name, width in _REF_SEGMENTS:
        ref_off[name] = (o, width)
        o += width
    my_off, o, unit_names = {}, 0, []
    for name in _MY_ORDER:
        my_off[name] = o
        o += ref_off[name][1]
        unit_names += [name] * (ref_off[name][1] // UNIT)
    return ref_off, my_off, o, tuple(unit_names)


_REF_OFF, _OFF, IN_WIDTH, _UNIT_NAMES = _layout()
N_COL_TILES = 3
TN = IN_WIDTH // N_COL_TILES
UNITS_PER_TILE = TN // UNIT


def _silu(x):
    return x * jax.nn.sigmoid(x)


def _params(sem):
    return pltpu.CompilerParams(dimension_semantics=sem, vmem_limit_bytes=VMEM_LIMIT)


def _mod_kernel(cond_ref, w_ref, b_ref, o_ref):
    s = _silu(cond_ref[...])
    o_ref[...] = jnp.dot(s, w_ref[...], precision=lax.Precision.HIGHEST,
                         preferred_element_type=F32) + b_ref[...]


def _modulation(cond8, w_mod, b_mod):
    b3 = b_mod.reshape(DEPTH, 1, 3 * D_MODEL)
    return pl.pallas_call(
        _mod_kernel,
        grid=(DEPTH, 3),
        in_specs=[pl.BlockSpec((8, D_MODEL), lambda l, j: (0, 0)),
                  pl.BlockSpec((None, D_MODEL, D_MODEL), lambda l, j: (l, 0, j)),
                  pl.BlockSpec((None, 1, D_MODEL), lambda l, j: (l, 0, j))],
        out_specs=pl.BlockSpec((None, 8, D_MODEL), lambda l, j: (l, 0, j)),
        out_shape=jax.ShapeDtypeStruct((DEPTH, 8, 3 * D_MODEL), F32),
        compiler_params=_params(("arbitrary", "arbitrary")),
        name="modulation",
    )(cond8, w_mod, b3)


def _rope(acc, cos_ref, sina_ref, sinb_ref):
    cos, sina, sinb = cos_ref[...], sina_ref[...], sinb_ref[...]
    outs = []
    for hd in range(acc.shape[1] // LANES):
        a = acc[:, hd * LANES:(hd + 1) * LANES]
        up = pltpu.roll(a, LANES - 16, 1)
        dn = pltpu.roll(a, 16, 1)
        outs.append(a * cos + up * sina + dn * sinb)
    return jnp.concatenate(outs, axis=1)


def _inproj_kernel(*refs, latent, tm, seq, tiles_per_batch, n_alias):
    if latent:
        (x_ref, mod_ref, gpre_ref, w_ref, cos_ref, sina_ref, sinb_ref, p_ref, h_ref) = refs
        cache_refs = {}
    else:
        (x_ref, mod_ref, gpre_ref, w_ref) = refs[:4]
        (p_ref, h_ref, dk_ref, dv_ref, nk_ref, nv_ref) = refs[4 + n_alias:]
        cache_refs = {"dk": dk_ref, "dv": dv_ref, "nk": nk_ref, "nv": nv_ref}
    i = pl.program_id(0)
    j = pl.program_id(1)

    @pl.when(j == 0)
    def _():
        x = x_ref[...]
        r = lax.rsqrt(jnp.mean(x * x, axis=-1, keepdims=True) + EPS)
        row = (1 + i // tiles_per_batch) if latent else 0
        m = mod_ref[pl.ds(row, 1), :]
        shift = m[:, :D_MODEL]
        scale = m[:, D_MODEL:2 * D_MODEL]
        h_ref[...] = ((x * r * gpre_ref[...]) * (1.0 + scale) + shift).astype(BF16)

    def unit(u, name):
        cols = slice(u * UNIT, (u + 1) * UNIT)
        acc = jnp.dot(h_ref[...], w_ref[:, cols], preferred_element_type=F32)
        if name in cache_refs:
            cache_refs[name][...] = acc.reshape(tm // seq, seq, UNIT)
        if name in ("dq", "nq"):
            acc = acc * (DIFF_HEAD_DIM ** -0.5)
        if latent and name in ("dq", "dk"):
            acc = _rope(acc, cos_ref, sina_ref, sinb_ref)
        p_ref[:, cols] = acc.astype(BF16)

    for jt in range(N_COL_TILES):
        @pl.when(j == jt)
        def _(jt=jt):
            for u in range(UNITS_PER_TILE):
                unit(u, _UNIT_NAMES[jt * UNITS_PER_TILE + u])


def _inproj(x2d, mod_all, g_pre3, w_in_b, l, *, latent, tm, seq, rope_tabs=None, cache_bufs=None):
    t = x2d.shape[0]
    n_i = t // tm
    in_specs = [pl.BlockSpec((tm, D_MODEL), lambda i, j: (i, 0)),
                pl.BlockSpec((None, 8, 3 * D_MODEL), lambda i, j: (l, 0, 0)),
                pl.BlockSpec((None, 1, D_MODEL), lambda i, j: (l, 0, 0)),
                pl.BlockSpec((None, D_MODEL, TN), lambda i, j: (l, 0, j))]
    args = [x2d, mod_all, g_pre3, w_in_b]
    out_specs = [pl.BlockSpec((tm, TN), lambda i, j: (i, j)),
                 pl.BlockSpec((tm, D_MODEL), lambda i, j: (i, 0))]
    out_shape = [jax.ShapeDtypeStruct((t, IN_WIDTH), BF16),
                 jax.ShapeDtypeStruct((t, D_MODEL), BF16)]
    tiles_per_batch = 1
    aliases = {}
    n_alias = 0
    if latent:
        tiles_per_batch = seq // tm
        for tab in rope_tabs:
            in_specs.append(pl.BlockSpec((tm, LANES), lambda i, j: (i % tiles_per_batch, 0)))
            args.append(tab)
    else:
        nb = tm // seq
        if cache_bufs is not None:
            n_alias = len(cache_bufs)
            for k, buf in enumerate(cache_bufs):
                in_specs.append(pl.BlockSpec(memory_space=pl.ANY))
                args.append(buf)
                aliases[4 + k] = 2 + k
        for _ in range(4):
            out_specs.append(pl.BlockSpec((nb, None, seq, UNIT), lambda i, j: (i, l, 0, 0)))
            out_shape.append(jax.ShapeDtypeStruct((t // seq, DEPTH, seq, UNIT), F32))
    return pl.pallas_call(
        functools.partial(_inproj_kernel, latent=latent, tm=tm, seq=seq, tiles_per_batch=tiles_per_batch,
                          n_alias=n_alias),
        grid=(n_i, N_COL_TILES),
        in_specs=in_specs,
        out_specs=out_specs,
        out_shape=out_shape,
        input_output_aliases=aliases,
        compiler_params=_params(("arbitrary", "arbitrary")),
        name="inproj_latent" if latent else "inproj_prompt",
    )(*args)


def _lane_tile_reduce(op, x):
    parts = [x[:, t * LANES:(t + 1) * LANES] for t in range(x.shape[1] // LANES)]
    return functools.reduce(op, parts)


def _softmax_pv(q, k_ref, v_ref, s_ref, kcols, n_keys, kc):
    dn = (((1,), (1,)), ((), ()))
    mpart = None
    for c in range(n_keys // kc):
        rows = slice(c * kc, (c + 1) * kc)
        s = lax.dot_general(q, k_ref[rows, kcols], dn, preferred_element_type=F32)
        s_ref[:, rows] = s
        mc = _lane_tile_reduce(jnp.maximum, s)
        mpart = mc if mpart is None else jnp.maximum(mpart, mc)
    m = jnp.max(mpart, axis=-1, keepdims=True)
    lpart = None
    o = None
    for c in range(n_keys // kc):
        rows = slice(c * kc, (c + 1) * kc)
        e = jnp.exp(s_ref[:, rows] - m)
        lc = _lane_tile_reduce(jnp.add, e)
        lpart = lc if lpart is None else lpart + lc
        oc = jnp.dot(e.astype(BF16), v_ref[rows, kcols], preferred_element_type=F32)
        o = oc if o is None else o + oc
    return o * (1.0 / jnp.sum(lpart, axis=-1, keepdims=True))


def _diff_attn_kernel(*refs, lam_init, n_self, has_ctx, heads, kc):
    if has_ctx:
        (lamp_ref, subg_ref, q_ref, k_ref, v_ref, gate_ref, ck_ref, cv_ref, o_ref,
         s1_ref, s2_ref, kc_ref, vc_ref) = refs

        @pl.when(pl.program_id(2) == 0)
        def _():
            kc_ref[0:n_self, :] = k_ref[...]
            vc_ref[0:n_self, :] = v_ref[...]
            kc_ref[n_self:, :] = ck_ref[...].astype(BF16)
            vc_ref[n_self:, :] = cv_ref[...].astype(BF16)

        k_ref, v_ref = kc_ref, vc_ref
    else:
        (lamp_ref, subg_ref, q_ref, k_ref, v_ref, gate_ref, o_ref, s1_ref, s2_ref) = refs
    n_keys = k_ref.shape[0]
    lp = lamp_ref[...]
    lam = (jnp.exp(jnp.sum(lp[0:1] * lp[1:2], axis=-1, keepdims=True))
           - jnp.exp(jnp.sum(lp[2:3] * lp[3:4], axis=-1, keepdims=True)) + lam_init)
    for hd in range(heads):
        cols = slice(hd * LANES, (hd + 1) * LANES)
        q = q_ref[:, cols]
        lane = lax.broadcasted_iota(jnp.int32, q.shape, 1)
        zero = jnp.zeros_like(q)
        o1 = _softmax_pv(jnp.where(lane < DIFF_HEAD_DIM, q, zero), k_ref, v_ref, s1_ref, cols, n_keys, kc)
        o2 = _softmax_pv(jnp.where(lane >= DIFF_HEAD_DIM, q, zero), k_ref, v_ref, s2_ref, cols, n_keys, kc)
        o = o1 - lam * o2
        y = o * lax.rsqrt(jnp.mean(o * o, axis=-1, keepdims=True) + EPS) * subg_ref[...]
        y = y * (1.0 - lam_init)
        o_ref[:, cols] = (y * _silu(gate_ref[:, cols].astype(F32))).astype(BF16)


def _diff_attention(p3, lamp, subg3, l, *, tq, heads, ctx=None):
    b, s, _ = p3.shape
    w = heads * LANES
    cq, ck, cv, cg = (_OFF[n] // w for n in ("dq", "dk", "dv", "d_gate"))
    lam_init = 0.8 - 0.6 * math.exp(-0.3 * l)
    in_specs = [pl.BlockSpec((None, 4, DIFF_HEAD_DIM), lambda bi, h, qi: (l, 0, 0)),
                pl.BlockSpec((None, 1, LANES), lambda bi, h, qi: (l, 0, 0)),
                pl.BlockSpec((None, tq, w), lambda bi, h, qi: (bi, qi, cq + h)),
                pl.BlockSpec((None, s, w), lambda bi, h, qi: (bi, 0, ck + h)),
                pl.BlockSpec((None, s, w), lambda bi, h, qi: (bi, 0, cv + h)),
                pl.BlockSpec((None, tq, w), lambda bi, h, qi: (bi, qi, cg + h))]
    args = [lamp, subg3, p3, p3, p3, p3]
    n_keys = s
    scratch = []
    if ctx is not None:
        ck4, cv4 = ctx
        n_ctx = ck4.shape[2]
        n_keys = s + n_ctx
        for arr in (ck4, cv4):
            in_specs.append(pl.BlockSpec((None, None, n_ctx, w), lambda bi, h, qi: (bi, l, 0, h)))
            args.append(arr)
        scratch = [pltpu.VMEM((n_keys, w), BF16), pltpu.VMEM((n_keys, w), BF16)]
    kc = min(n_keys, 512)
    scratch = [pltpu.VMEM((tq, n_keys), F32), pltpu.VMEM((tq, n_keys), F32)] + scratch
    return pl.pallas_call(
        functools.partial(_diff_attn_kernel, lam_init=lam_init, n_self=s, has_ctx=ctx is not None,
                          heads=heads, kc=kc),
        grid=(b, DIFF_HEADS // heads, s // tq),
        in_specs=in_specs,
        out_specs=pl.BlockSpec((None, tq, w), lambda bi, h, qi: (bi, qi, h)),
        out_shape=jax.ShapeDtypeStruct((b, s, DIFF_HEADS * LANES), BF16),
        scratch_shapes=scratch,
        compiler_params=_params(("arbitrary", "arbitrary", "arbitrary")),
        name="diff_attn_latent" if ctx is not None else "diff_attn_prompt",
    )(*args)


def _softmax_rows(s):
    m = jnp.max(s, axis=-1, keepdims=True)
    e = jnp.exp(s - m)
    return e * (1.0 / jnp.sum(e, axis=-1, keepdims=True))


def _ctx_attn_kernel(q_ref, k_ref, v_ref, gate_ref, o_ref):
    dn = (((1,), (1,)), ((), ()))
    for hp in range(NA_HEADS // 2):
        cols = slice(hp * LANES, (hp + 1) * LANES)
        q, k, v = q_ref[:, cols], k_ref[:, cols], v_ref[:, cols]
        lane = lax.broadcasted_iota(jnp.int32, q.shape, 1)
        zero = jnp.zeros_like(q)
        outs = []
        for hh in range(2):
            sel = (lane < NA_HEAD_DIM) if hh == 0 else (lane >= NA_HEAD_DIM)
            qh = jnp.where(sel, q, zero)
            p = _softmax_rows(lax.dot_general(qh, k, dn, preferred_element_type=F32)).astype(BF16)
            outs.append(jnp.dot(p, v, preferred_element_type=F32))
        o = jnp.where(lane < NA_HEAD_DIM, outs[0], outs[1])
        o_ref[:, cols] = (o * _silu(gate_ref[:, cols].astype(F32))).astype(BF16)


def _context_attention(p3):
    b, s, _ = p3.shape
    w = NA_HEADS * NA_HEAD_DIM
    cq, ck, cv, cg = (_OFF[n] // w for n in ("nq", "nk", "nv", "n_gate"))
    spec = lambda c: pl.BlockSpec((None, s, w), lambda bi: (bi, 0, c))
    return pl.pallas_call(
        _ctx_attn_kernel,
        grid=(b,),
        in_specs=[spec(cq), spec(ck), spec(cv), spec(cg)],
        out_specs=pl.BlockSpec((None, s, w), lambda bi: (bi, 0, 0)),
        out_shape=jax.ShapeDtypeStruct((b, s, w), BF16),
        compiler_params=_params(("arbitrary",)),
        name="ctx_attn_prompt",
    )(p3, p3, p3, p3)


def _na_attn_kernel(q_ref, k_ref, v_ref, gate_ref, ck_ref, cv_ref, bias_ref, o_ref,
                    sctx_ref, pctx_ref, oloc_ref, *, rows_per_step):
    rb = pl.program_id(2)
    ckb = ck_ref[...].astype(BF16)
    cvb = cv_ref[...].astype(BF16)
    n_loc = NA_WIN_R * GRID_W
    dn = (((1,), (1,)), ((), ()))
    q_all = q_ref[...]
    lane_all = lax.broadcasted_iota(jnp.int32, q_all.shape, 1)
    zero_all = jnp.zeros_like(q_all)
    for hh in range(2):
        sel = (lane_all < NA_HEAD_DIM) if hh == 0 else (lane_all >= NA_HEAD_DIM)
        sctx_ref[hh] = lax.dot_general(jnp.where(sel, q_all, zero_all), ckb, dn, preferred_element_type=F32)
    lane = lax.broadcasted_iota(jnp.int32, (GRID_W, LANES), 1)

    def row_body(rl, carry):
        r = rb * rows_per_step + rl
        r0 = jnp.clip(r - NA_WIN_R // 2, 0, GRID_H - NA_WIN_R)
        start = r0 - r + (NA_WIN_R - 1)
        qoff = pl.multiple_of(rl * GRID_W, GRID_W)
        koff = pl.multiple_of(r0 * GRID_W, GRID_W)
        q = q_ref[pl.ds(qoff, GRID_W), :]
        zero = jnp.zeros_like(q)
        qs = jnp.concatenate([jnp.where(lane < NA_HEAD_DIM, q, zero),
                              jnp.where(lane >= NA_HEAD_DIM, q, zero)], axis=0)
        kl = k_ref[pl.ds(koff, n_loc), :]
        vl = v_ref[pl.ds(koff, n_loc), :]
        bias = jnp.concatenate([bias_ref[start, 0], bias_ref[start, 1]], axis=0)
        s_loc = lax.dot_general(qs, kl, dn, preferred_element_type=F32) + bias
        s_ctx = jnp.concatenate([sctx_ref[0, pl.ds(qoff, GRID_W), :], sctx_ref[1, pl.ds(qoff, GRID_W), :]], axis=0)
        m = jnp.maximum(jnp.max(s_loc, axis=-1, keepdims=True), jnp.max(s_ctx, axis=-1, keepdims=True))
        e_loc = jnp.exp(s_loc - m)
        e_ctx = jnp.exp(s_ctx - m)
        inv = 1.0 / (jnp.sum(e_loc, axis=-1, keepdims=True) + jnp.sum(e_ctx, axis=-1, keepdims=True))
        p_ctx = (e_ctx * inv).astype(BF16)
        pctx_ref[0, pl.ds(qoff, GRID_W), :] = p_ctx[:GRID_W]
        pctx_ref[1, pl.ds(qoff, GRID_W), :] = p_ctx[GRID_W:]
        o2 = jnp.dot((e_loc * inv).astype(BF16), vl, preferred_element_type=F32)
        oloc_ref[pl.ds(qoff, GRID_W), :] = jnp.where(lane < NA_HEAD_DIM, o2[:GRID_W], o2[GRID_W:])
        return carry

    lax.fori_loop(0, rows_per_step, row_body, 0, unroll=2)
    o_ctx = jnp.where(lane_all < NA_HEAD_DIM,
                      jnp.dot(pctx_ref[0], cvb, preferred_element_type=F32),
                      jnp.dot(pctx_ref[1], cvb, preferred_element_type=F32))
    o = oloc_ref[...] + o_ctx
    o_ref[...] = (o * _silu(gate_ref[...].astype(F32))).astype(BF16)


def _na_bias_tables(rpb):
    c = np.arange(GRID_W)
    c0 = np.clip(c - NA_WIN_C // 2, 0, GRID_W - NA_WIN_C)
    in_win = (c[None, :] >= c0[:, None]) & (c[None, :] < c0[:, None] + NA_WIN_C)
    period = 2 * GRID_W - 1
    lead = rpb.shape[:-1]
    w = jnp.concatenate([rpb[..., NA_WIN_C - 1:], jnp.zeros(lead + (period - (2 * NA_WIN_C - 1),), rpb.dtype),
                         rpb[..., :NA_WIN_C - 1]], axis=-1)
    flat = jnp.tile(w, (1,) * len(lead) + (GRID_W,))[..., :GRID_W * (period - 1)]
    toe = flat.reshape(lead + (GRID_W, period - 1))[..., :GRID_W]
    toe = jnp.where(in_win, toe.astype(F32), -1e30)
    tabs = jnp.stack([toe[:, :, s:s + NA_WIN_R] for s in range(NA_WIN_R)], axis=1)
    tabs = jnp.transpose(tabs, (0, 1, 2, 4, 3, 5))
    return tabs.reshape(DEPTH, NA_WIN_R, NA_HEADS, GRID_W, NA_WIN_R * GRID_W)


def _neighbourhood_attention(p3, ck4, cv4, bias, l, *, rows_per_step=8):
    b, s, _ = p3.shape
    cq, ck, cv, cg = (_OFF[n] // LANES for n in ("nq", "nk", "nv", "n_gate"))
    n_ctx = ck4.shape[2]
    tq = rows_per_step * GRID_W
    n_loc = NA_WIN_R * GRID_W
    return pl.pallas_call(
        functools.partial(_na_attn_kernel, rows_per_step=rows_per_step),
        grid=(b, NA_HEADS // 2, GRID_H // rows_per_step),
        in_specs=[pl.BlockSpec((None, tq, LANES), lambda bi, hp, rb: (bi, rb, cq + hp)),
                  pl.BlockSpec((None, s, LANES), lambda bi, hp, rb: (bi, 0, ck + hp)),
                  pl.BlockSpec((None, s, LANES), lambda bi, hp, rb: (bi, 0, cv + hp)),
                  pl.BlockSpec((None, tq, LANES), lambda bi, hp, rb: (bi, rb, cg + hp)),
                  pl.BlockSpec((None, None, n_ctx, LANES), lambda bi, hp, rb: (bi, l, 0, hp)),
                  pl.BlockSpec((None, None, n_ctx, LANES), lambda bi, hp, rb: (bi, l, 0, hp)),
                  pl.BlockSpec((None, NA_WIN_R, 2, GRID_W, n_loc), lambda bi, hp, rb: (l, 0, hp, 0, 0))],
        out_specs=pl.BlockSpec((None, tq, LANES), lambda bi, hp, rb: (bi, rb, hp)),
        out_shape=jax.ShapeDtypeStruct((b, s, NA_HEADS * NA_HEAD_DIM), BF16),
        scratch_shapes=[pltpu.VMEM((2, tq, n_ctx), F32), pltpu.VMEM((2, tq, n_ctx), BF16),
                        pltpu.VMEM((tq, LANES), F32)],
        compiler_params=_params(("arbitrary", "arbitrary", "arbitrary")),
        name="na_attn_latent",
    )(p3, p3, p3, p3, ck4, cv4, bias)


def _dft_tables():
    gd = FNET_GROUP_DIM
    n = np.arange(gd)
    ang = 2.0 * np.pi * ((n[:, None] * n[None, :]) % gd) / gd
    fc = np.concatenate([np.cos(ang), -np.sin(ang)], axis=1) / 16.0
    fl = np.concatenate([np.cos(ang), np.sin(ang)], axis=1) / 16.0
    npos = GRID_H * GRID_W
    k1 = np.arange(64)[None, :, None]
    col = np.arange(64)[:, None, None]
    row = np.arange(64)[None, None, :]
    a1 = 2.0 * np.pi * ((k1 * (64 * row + col)) % npos) / npos
    t1 = np.concatenate([np.cos(a1), np.sin(a1)], axis=2) / 8.0
    k = np.arange(64)
    a2 = 2.0 * np.pi * ((k[:, None] * k[None, :]) % 64) / 64.0
    t2 = np.concatenate([np.cos(a2), np.sin(a2)], axis=1) / 8.0
    cast = lambda a: jnp.asarray(a.astype(np.float32)).astype(BF16)
    return cast(fc), cast(fl), cast(t1), cast(t2)


def _fft_prompt_kernel(h_ref, fg_ref, fc_ref, fl_ref, o_ref):
    gd = FNET_GROUP_DIM
    for g in range(FNET_GROUPS):
        cols = slice(g * gd, (g + 1) * gd)
        z = jnp.dot(h_ref[:, cols], fc_ref[...], preferred_element_type=F32)
        stack = jnp.concatenate([z[:, :gd], z[:, gd:]], axis=0).astype(BF16)
        y = jnp.dot(fl_ref[...], stack, preferred_element_type=F32)
        o_ref[:, cols] = (y * _silu(fg_ref[:, cols].astype(F32))).astype(BF16)


def _fourier_prompt(h2d, p2d, fc, fl, b, s):
    gd = FNET_GROUP_DIM
    cf = _OFF["f_gate"] // D_MODEL
    return pl.pallas_call(
        _fft_prompt_kernel,
        grid=(b,),
        in_specs=[pl.BlockSpec((s, D_MODEL), lambda bi: (bi, 0)),
                  pl.BlockSpec((s, D_MODEL), lambda bi: (bi, cf)),
                  pl.BlockSpec((gd, 2 * gd), lambda bi: (0, 0)),
                  pl.BlockSpec((s, 2 * s), lambda bi: (0, 0))],
        out_specs=pl.BlockSpec((s, D_MODEL), lambda bi: (bi, 0)),
        out_shape=jax.ShapeDtypeStruct((b * s, D_MODEL), BF16),
        compiler_params=_params(("arbitrary",)),
        name="fourier_prompt",
    )(h2d, p2d, fc, fl)


def _fft_latent_kernel(h_ref, fg_ref, fc_ref, t1_ref, t2_ref, o_ref, z_ref, a_ref, y_ref):
    gd = FNET_GROUP_DIM
    n = GRID_H * GRID_W
    rows = 512
    nt = 2 * gd // LANES
    for c in range(n // rows):
        sl = slice(c * rows, (c + 1) * rows)
        z = jnp.dot(h_ref[sl, :], fc_ref[...], preferred_element_type=F32)
        for t in range(nt):
            z_ref[t, sl, :] = z[:, t * LANES:(t + 1) * LANES]

    def stage1(col, carry):
        tiles = [z_ref[t, pl.ds(col, GRID_H, stride=GRID_W), :] for t in range(nt)]
        zr = jnp.concatenate(tiles[:nt // 2], axis=1)
        zi = jnp.concatenate(tiles[nt // 2:], axis=1)
        stack = jnp.concatenate([jnp.concatenate([zr, zi], axis=1),
                                 jnp.concatenate([zi, -zr], axis=1)], axis=0).astype(BF16)
        a = jnp.dot(t1_ref[col], stack, preferred_element_type=F32)
        off = pl.multiple_of(col * GRID_H, GRID_H)
        for t in range(nt):
            a_ref[t, pl.ds(off, GRID_H), :] = a[:, t * LANES:(t + 1) * LANES]
        return carry

    lax.fori_loop(0, GRID_W, stage1, 0, unroll=4)

    def stage2(k1, carry):
        tiles = [a_ref[t, pl.ds(k1, GRID_W, stride=GRID_H), :] for t in range(nt)]
        stack = jnp.concatenate([jnp.concatenate(tiles[:nt // 2], axis=1),
                                 jnp.concatenate(tiles[nt // 2:], axis=1)], axis=0).astype(BF16)
        y = jnp.dot(t2_ref[...], stack, preferred_element_type=F32)
        for t in range(nt // 2):
            y_ref[t, pl.ds(k1, GRID_H, stride=GRID_W), :] = y[:, t * LANES:(t + 1) * LANES]
        return carry

    lax.fori_loop(0, GRID_H, stage2, 0, unroll=4)
    for c in range(n // rows):
        sl = slice(c * rows, (c + 1) * rows)
        y = jnp.concatenate([y_ref[t, sl, :] for t in range(nt // 2)], axis=1)
        o_ref[sl, :] = (y * _silu(fg_ref[sl, :].astype(F32))).astype(BF16)


def _fourier_latent(h2d, p2d, fc, t1, t2, b):
    gd = FNET_GROUP_DIM
    n = GRID_H * GRID_W
    cf = _OFF["f_gate"] // gd
    return pl.pallas_call(
        _fft_latent_kernel,
        grid=(b, FNET_GROUPS),
        in_specs=[pl.BlockSpec((n, gd), lambda bi, g: (bi, g)),
                  pl.BlockSpec((n, gd), lambda bi, g: (bi, cf + g)),
                  pl.BlockSpec((gd, 2 * gd), lambda bi, g: (0, 0)),
                  pl.BlockSpec((GRID_W, GRID_H, 2 * GRID_H), lambda bi, g: (0, 0, 0)),
                  pl.BlockSpec((GRID_H, 2 * GRID_W), lambda bi, g: (0, 0))],
        out_specs=pl.BlockSpec((n, gd), lambda bi, g: (bi, g)),
        out_shape=jax.ShapeDtypeStruct((b * n, D_MODEL), BF16),
        scratch_shapes=[pltpu.VMEM((2 * gd // LANES, n, LANES), F32), pltpu.VMEM((2 * gd // LANES, n, LANES), F32),
                        pltpu.VMEM((gd // LANES, n, LANES), F32)],
        compiler_params=_params(("arbitrary", "arbitrary")),
        name="fourier_latent",
    )(h2d, p2d, fc, t1, t2)


def _merge_kernel(x_ref, mod_ref, mg_ref, u_ref, v_ref, sgate_ref, yd_ref, yn_ref, yf_ref,
                  gv_ref, sw_ref, sb_ref, psg_ref, pd_ref, pn_ref, pf_ref, wo_ref, gpost_ref,
                  o_ref, ysg_ref, *, latent, tm, tiles_per_batch):
    i = pl.program_id(0)
    lane = lax.broadcasted_iota(jnp.int32, (CHUNK, LANES), 1)
    for c in range(tm // CHUNK):
        sl = slice(c * CHUNK, (c + 1) * CHUNK)
        v = v_ref[sl, :].astype(F32)
        vn = (v * lax.rsqrt(jnp.mean(v * v, axis=-1, keepdims=True) + EPS) * gv_ref[...]).astype(BF16)
        parts = []
        for pr in range(SG_GROUPS // 2):
            vp = vn[:, pr * LANES:(pr + 1) * LANES]
            s0 = jnp.dot(sw_ref[2 * pr], vp, preferred_element_type=F32)
            s1 = jnp.dot(sw_ref[2 * pr + 1], vp, preferred_element_type=F32)
            parts.append(jnp.where(lane < SG_WIDTH // SG_GROUPS, s0, s1))
        s = jnp.concatenate(parts, axis=1) + sb_ref[...]
        y = u_ref[sl, :].astype(F32) * s * _silu(sgate_ref[sl, :].astype(F32))
        ysg_ref[sl, :] = y.astype(BF16)

    def gate(n):
        return jax.nn.sigmoid(mg_ref[:, n * D_MODEL:(n + 1) * D_MODEL].astype(F32))

    mixed = gate(0) * jnp.dot(ysg_ref[...], psg_ref[...], preferred_element_type=F32)
    mixed = mixed + gate(1) * jnp.dot(yd_ref[...], pd_ref[...], preferred_element_type=F32)
    mixed = mixed + gate(2) * jnp.dot(yn_ref[...], pn_ref[...], preferred_element_type=F32)
    mixed = mixed + gate(3) * jnp.dot(yf_ref[...], pf_ref[...], preferred_element_type=F32)
    out = jnp.dot(mixed.astype(BF16), wo_ref[...], preferred_element_type=F32)
    out = out * lax.rsqrt(jnp.mean(out * out, axis=-1, keepdims=True) + EPS) * gpost_ref[...]
    row = (1 + i // tiles_per_batch) if latent else 0
    g = mod_ref[pl.ds(row, 1), :][:, 2 * D_MODEL:]
    o_ref[...] = x_ref[...] + g * out


def _merge(x2d, mod_all, p2d, yd, yn, yf, w, l, *, latent, seq, tm=256):
    t = x2d.shape[0]
    cu, cv, cs = (_OFF[n] // SG_WIDTH for n in ("sg_u", "sg_v", "sg_gate"))
    tok = lambda width, c=0: pl.BlockSpec((tm, width), lambda i: (i, c))
    wspec = lambda *shape: pl.BlockSpec((None,) + shape, lambda i: (l,) + (0,) * len(shape))
    tiles_per_batch = seq // tm if latent else 1
    return pl.pallas_call(
        functools.partial(_merge_kernel, latent=latent, tm=tm, tiles_per_batch=tiles_per_batch),
        grid=(t // tm,),
        in_specs=[tok(D_MODEL), wspec(8, 3 * D_MODEL), tok(N_BRANCH * D_MODEL, 0),
                  tok(SG_WIDTH, cu), tok(SG_WIDTH, cv), tok(SG_WIDTH, cs),
                  tok(512), tok(512), tok(D_MODEL),
                  wspec(1, SG_WIDTH), wspec(SG_GROUPS, CHUNK, CHUNK), wspec(CHUNK, SG_WIDTH),
                  wspec(SG_WIDTH, D_MODEL), wspec(512, D_MODEL), wspec(512, D_MODEL),
                  wspec(D_MODEL, D_MODEL), wspec(D_MODEL, D_MODEL), wspec(1, D_MODEL)],
        out_specs=tok(D_MODEL),
        out_shape=jax.ShapeDtypeStruct((t, D_MODEL), F32),
        scratch_shapes=[pltpu.VMEM((tm, SG_WIDTH), BF16)],
        compiler_params=_params(("arbitrary",)),
        name="merge_latent" if latent else "merge_prompt",
    )(x2d, mod_all, p2d, p2d, p2d, p2d, yd, yn, yf,
      w["sg_norm_g"], w["sg_w"], w["sg_b"], w["p_sg"], w["p_diff"], w["p_na"], w["p_fnet"], w["w_out"],
      w["g_post"])


def _rope_tables(n_tokens):
    nf = DIFF_HEAD_DIM // 4
    t = np.arange(n_tokens)
    inv = ROPE_BASE ** (-np.arange(nf, dtype=np.float64) / nf)
    ang_r = (t // GRID_W).astype(np.float64)[:, None] * inv[None, :]
    ang_c = (t % GRID_W).astype(np.float64)[:, None] * inv[None, :]
    z = np.zeros_like(ang_r)
    cos64 = np.concatenate([np.cos(ang_r), np.cos(ang_r), np.cos(ang_c), np.cos(ang_c)], axis=1)
    sina64 = np.concatenate([-np.sin(ang_r), z, -np.sin(ang_c), z], axis=1)
    sinb64 = np.concatenate([z, np.sin(ang_r), z, np.sin(ang_c)], axis=1)
    two = lambda a: jnp.asarray(np.concatenate([a, a], axis=1).astype(np.float32))
    return two(cos64), two(sina64), two(sinb64)


def kernel(x_prompt, x_sample, cache_diff_k, cache_diff_v, cache_na_k, cache_na_v, c, c_ctx,
           w_mod, b_mod, g_pre, g_post, w_in, sg_norm_g, sg_w, sg_b,
           diff_lam_q1, diff_lam_k1, diff_lam_q2, diff_lam_k2, diff_subln_g, na_rpb,
           w_proj_sg, w_proj_diff, w_proj_na, w_proj_fnet, w_out):
    bp, sp, _ = x_prompt.shape
    bs, ss, _ = x_sample.shape
    n_ctx = cache_diff_k.shape[2]

    cond8 = jnp.zeros((8, D_MODEL), F32).at[0].set(c_ctx).at[1:1 + bs].set(c)
    mod_all = _modulation(cond8, w_mod, b_mod)

    w_in_b = jnp.concatenate(
        [w_in[:, :, _REF_OFF[n][0]:_REF_OFF[n][0] + _REF_OFF[n][1]] for n in _MY_ORDER], axis=-1).astype(BF16)
    weights = {
        "sg_norm_g": sg_norm_g.reshape(DEPTH, 1, SG_WIDTH),
        "sg_w": sg_w.astype(BF16),
        "sg_b": jnp.repeat(jnp.transpose(sg_b, (0, 2, 1)), SG_WIDTH // SG_GROUPS, axis=2),
        "p_sg": w_proj_sg.astype(BF16), "p_diff": w_proj_diff.astype(BF16),
        "p_na": w_proj_na.astype(BF16), "p_fnet": w_proj_fnet.astype(BF16),
        "w_out": w_out.astype(BF16), "g_post": g_post.reshape(DEPTH, 1, D_MODEL),
    }
    g_pre3 = g_pre.reshape(DEPTH, 1, D_MODEL)
    lamp = jnp.stack([diff_lam_q1, diff_lam_k1, diff_lam_q2, diff_lam_k2], axis=1)
    subg3 = diff_subln_g.reshape(DEPTH, 1, 2 * DIFF_HEAD_DIM)
    rope_tabs = _rope_tables(ss)
    fc, fl, t1, t2 = _dft_tables()
    bias = _na_bias_tables(na_rpb)
    ckd = cache_diff_k.reshape(bs, DEPTH, n_ctx, DIFF_HEADS * 2 * DIFF_HEAD_DIM)
    cvd = cache_diff_v.reshape(bs, DEPTH, n_ctx, DIFF_HEADS * 2 * DIFF_HEAD_DIM)
    ckn = cache_na_k.reshape(bs, DEPTH, n_ctx, NA_HEADS * NA_HEAD_DIM)
    cvn = cache_na_v.reshape(bs, DEPTH, n_ctx, NA_HEADS * NA_HEAD_DIM)

    xp = x_prompt.reshape(bp * sp, D_MODEL)
    xs = x_sample.reshape(bs * ss, D_MODEL)
    caches = None
    for l in range(DEPTH):
        pp, hp, *caches = _inproj(xp, mod_all, g_pre3, w_in_b, l, latent=False, tm=2 * sp, seq=sp,
                                  cache_bufs=caches)
        pp3 = pp.reshape(bp, sp, IN_WIDTH)
        yd = _diff_attention(pp3, lamp, subg3, l, tq=sp, heads=DIFF_HEADS).reshape(bp * sp, -1)
        yn = _context_attention(pp3).reshape(bp * sp, -1)
        yf = _fourier_prompt(hp, pp, fc, fl, bp, sp)
        xp = _merge(xp, mod_all, pp, yd, yn, yf, weights, l, latent=False, seq=sp)
        ps, hs = _inproj(xs, mod_all, g_pre3, w_in_b, l, latent=True, tm=1024, seq=ss, rope_tabs=rope_tabs)
        ps3 = ps.reshape(bs, ss, IN_WIDTH)
        yd = _diff_attention(ps3, lamp, subg3, l, tq=256, heads=1, ctx=(ckd, cvd)).reshape(bs * ss, -1)
        yn = _neighbourhood_attention(ps3, ckn, cvn, bias, l).reshape(bs * ss, -1)
        yf = _fourier_latent(hs, ps, fc, t1, t2, bs)
        xs = _merge(xs, mod_all, ps, yd, yn, yf, weights, l, latent=True, seq=ss)

    dk5, dv5, nk5, nv5 = caches
    return (xp.reshape(bp, sp, D_MODEL), xs.reshape(bs, ss, D_MODEL),
            dk5.reshape(bp, DEPTH, sp, DIFF_HEADS, 2 * DIFF_HEAD_DIM),
            dv5.reshape(bp, DEPTH, sp, DIFF_HEADS, 2 * DIFF_HEAD_DIM),
            nk5.reshape(bp, DEPTH, sp, NA_HEADS, NA_HEAD_DIM),
            nv5.reshape(bp, DEPTH, sp, NA_HEADS, NA_HEAD_DIM))
```

```python
import functools
import math

import numpy as np
import jax
import jax.numpy as jnp
from jax import lax
from jax.experimental import pallas as pl
from jax.experimental.pallas import tpu as pltpu

F32 = jnp.float32
BF16 = jnp.bfloat16

D_MODEL = 1024
DEPTH = 4
GRID_W = 64
GRID_H = 64
EPS = 1e-6
ROPE_BASE = 10000.0
CHUNK = 128
SG_WIDTH = 512
SG_GROUPS = 8
DIFF_HEADS = 4
DIFF_HEAD_DIM = 64
NA_HEADS = 8
NA_HEAD_DIM = 64
NA_WIN_R = 8
NA_WIN_C = 16
FNET_GROUPS = 4
FNET_GROUP_DIM = 256
N_BRANCH = 4
LANES = 128
VMEM_LIMIT = 56 * 1024 * 1024
LOG2E = math.log2(math.e)

_REF_SEGMENTS = (("sg_u", 512), ("sg_v", 512), ("sg_gate", 512),
                 ("dq", 512), ("dk", 512), ("dv", 512), ("d_gate", 512),
                 ("nq", 512), ("nk", 512), ("nv", 512), ("n_gate", 512),
                 ("f_gate", 1024), ("merge", 4096))
_MY_ORDER = ("merge", "f_gate", "sg_u", "sg_v", "sg_gate", "dq", "dk", "dv", "d_gate",
             "nq", "nk", "nv", "n_gate")
UNIT = 512


def _layout():
    ref_off, o = {}, 0
    for name, width in _REF_SEGMENTS:
        ref_off[name] = (o, width)
        o += width
    my_off, o, unit_names = {}, 0, []
    for name in _MY_ORDER:
        my_off[name] = o
        o += ref_off[name][1]
        unit_names += [name] * (ref_off[name][1] // UNIT)
    return ref_off, my_off, o, tuple(unit_names)


_REF_OFF, _OFF, IN_WIDTH, _UNIT_NAMES = _layout()


def _silu(x):
    return x * jax.nn.sigmoid(x)


def _params(sem):
    return pltpu.CompilerParams(dimension_semantics=sem, vmem_limit_bytes=VMEM_LIMIT)


def _mod_kernel(cond_ref, w_ref, b_ref, o_ref):
    s = _silu(cond_ref[...])
    o_ref[...] = jnp.dot(s, w_ref[...], precision=lax.Precision.HIGHEST,
                         preferred_element_type=F32) + b_ref[...]


def _modulation(cond8, w_mod, b_mod):
    b3 = b_mod.reshape(DEPTH, 1, 3 * D_MODEL)
    return pl.pallas_call(
        _mod_kernel,
        grid=(DEPTH, 3),
        in_specs=[pl.BlockSpec((8, D_MODEL), lambda l, j: (0, 0)),
                  pl.BlockSpec((None, D_MODEL, D_MODEL), lambda l, j: (l, 0, j)),
                  pl.BlockSpec((None, 1, D_MODEL), lambda l, j: (l, 0, j))],
        out_specs=pl.BlockSpec((None, 8, D_MODEL), lambda l, j: (l, 0, j)),
        out_shape=jax.ShapeDtypeStruct((DEPTH, 8, 3 * D_MODEL), F32),
        compiler_params=_params(("arbitrary", "arbitrary")),
        name="modulation",
    )(cond8, w_mod, b3)


def _rope(acc, cos_ref, sina_ref, sinb_ref):
    cos, sina, sinb = cos_ref[...], sina_ref[...], sinb_ref[...]
    outs = []
    for hd in range(acc.shape[1] // LANES):
        a = acc[:, hd * LANES:(hd + 1) * LANES]
        up = pltpu.roll(a, LANES - 16, 1)
        dn = pltpu.roll(a, 16, 1)
        outs.append(a * cos + up * sina + dn * sinb)
    return jnp.concatenate(outs, axis=1)


def _inproj_kernel(*refs, latent, tm, seq, tiles_per_batch, n_alias, n_col_tiles):
    if latent:
        (x_ref, mod_ref, gpre_ref, w_ref, cos_ref, sina_ref, sinb_ref, p_ref, h_ref) = refs
        cache_refs = {}
    else:
        (x_ref, mod_ref, gpre_ref, w_ref) = refs[:4]
        (p_ref, h_ref, dk_ref, dv_ref, nk_ref, nv_ref) = refs[4 + n_alias:]
        cache_refs = {"dk": dk_ref, "dv": dv_ref, "nk": nk_ref, "nv": nv_ref}
    i = pl.program_id(0)
    j = pl.program_id(1)

    @pl.when(j == 0)
    def _():
        x = x_ref[...]
        r = lax.rsqrt(jnp.mean(x * x, axis=-1, keepdims=True) + EPS)
        row = (1 + i // tiles_per_batch) if latent else 0
        m = mod_ref[pl.ds(row, 1), :]
        shift = m[:, :D_MODEL]
        scale = m[:, D_MODEL:2 * D_MODEL]
        h_ref[...] = ((x * r * gpre_ref[...]) * (1.0 + scale) + shift).astype(BF16)

    def unit(u, name):
        cols = slice(u * UNIT, (u + 1) * UNIT)
        acc = jnp.dot(h_ref[...], w_ref[:, cols], preferred_element_type=F32)
        if name in cache_refs:
            cache_refs[name][...] = acc.reshape(tm // seq, seq, UNIT)
        if name in ("dq", "nq"):
            acc = acc * (DIFF_HEAD_DIM ** -0.5 * LOG2E)
        if latent and name in ("dq", "dk"):
            acc = _rope(acc, cos_ref, sina_ref, sinb_ref)
        p_ref[:, cols] = acc.astype(BF16)

    units_per_tile = len(_UNIT_NAMES) // n_col_tiles
    for jt in range(n_col_tiles):
        @pl.when(j == jt)
        def _(jt=jt):
            for u in range(units_per_tile):
                unit(u, _UNIT_NAMES[jt * units_per_tile + u])


def _inproj(x2d, mod_all, g_pre3, w_in_b, l, *, latent, tm, seq, n_col_tiles, rope_tabs=None, cache_bufs=None):
    t = x2d.shape[0]
    n_i = t // tm
    tn = IN_WIDTH // n_col_tiles
    in_specs = [pl.BlockSpec((tm, D_MODEL), lambda i, j: (i, 0)),
                pl.BlockSpec((None, 8, 3 * D_MODEL), lambda i, j: (l, 0, 0)),
                pl.BlockSpec((None, 1, D_MODEL), lambda i, j: (l, 0, 0)),
                pl.BlockSpec((None, D_MODEL, tn), lambda i, j: (l, 0, j))]
    args = [x2d, mod_all, g_pre3, w_in_b]
    out_specs = [pl.BlockSpec((tm, tn), lambda i, j: (i, j)),
                 pl.BlockSpec((tm, D_MODEL), lambda i, j: (i, 0))]
    out_shape = [jax.ShapeDtypeStruct((t, IN_WIDTH), BF16),
                 jax.ShapeDtypeStruct((t, D_MODEL), BF16)]
    tiles_per_batch = 1
    aliases = {}
    n_alias = 0
    if latent:
        tiles_per_batch = seq // tm
        for tab in rope_tabs:
            in_specs.append(pl.BlockSpec((tm, LANES), lambda i, j: (i % tiles_per_batch, 0)))
            args.append(tab)
    else:
        nb = tm // seq
        if cache_bufs is not None:
            n_alias = len(cache_bufs)
            for k, buf in enumerate(cache_bufs):
                in_specs.append(pl.BlockSpec(memory_space=pl.ANY))
                args.append(buf)
                aliases[4 + k] = 2 + k
        for _ in range(4):
            out_specs.append(pl.BlockSpec((nb, None, seq, UNIT), lambda i, j: (i, l, 0, 0)))
            out_shape.append(jax.ShapeDtypeStruct((t // seq, DEPTH, seq, UNIT), F32))
    return pl.pallas_call(
        functools.partial(_inproj_kernel, latent=latent, tm=tm, seq=seq, tiles_per_batch=tiles_per_batch,
                          n_alias=n_alias, n_col_tiles=n_col_tiles),
        grid=(n_i, n_col_tiles),
        in_specs=in_specs,
        out_specs=out_specs,
        out_shape=out_shape,
        input_output_aliases=aliases,
        compiler_params=_params(("arbitrary", "arbitrary")),
        name="inproj_latent" if latent else "inproj_prompt",
    )(*args)


def _lane_tile_reduce(op, x):
    parts = [x[:, t * LANES:(t + 1) * LANES] for t in range(x.shape[1] // LANES)]
    return functools.reduce(op, parts)


def _qk_chunk(q, k_ref, s_ref, kcols, c, kc):
    rows = slice(c * kc, (c + 1) * kc)
    s = lax.dot_general(q, k_ref[rows, kcols], (((1,), (1,)), ((), ())), preferred_element_type=F32)
    s_ref[:, rows] = s
    return _lane_tile_reduce(jnp.maximum, s)


def _pv_chunk(s_ref, m, vaug, c, kc):
    rows = slice(c * kc, (c + 1) * kc)
    e = jnp.exp2(s_ref[:, rows] - m).astype(BF16)
    return jnp.dot(e, vaug[rows], preferred_element_type=F32)


def _pipelined_attention(streams, k_ref, s_refs, n_keys, kc):
    nck = n_keys // kc
    acc = lambda a, b: b if a is None else a + b
    mx = lambda a, b: b if a is None else jnp.maximum(a, b)
    outs = []
    if nck == 1:
        for q, cols, vaug in streams:
            s = lax.dot_general(q, k_ref[:, cols], (((1,), (1,)), ((), ())), preferred_element_type=F32)
            e = jnp.exp2(s - jnp.max(s, axis=-1, keepdims=True)).astype(BF16)
            r = jnp.dot(e, vaug, preferred_element_type=F32)
            outs.append(r[:, :LANES] * (1.0 / r[:, LANES:]))
        return outs
    mp = None
    for c in range(nck):
        mp = mx(mp, _qk_chunk(streams[0][0], k_ref, s_refs[0], streams[0][1], c, kc))
    for i, (_, _, vaug) in enumerate(streams):
        m = jnp.max(mp, axis=-1, keepdims=True)
        r = mp = None
        for c in range(nck):
            r = acc(r, _pv_chunk(s_refs[i % 2], m, vaug, c, kc))
            if i + 1 < len(streams):
                q_n, cols_n, _ = streams[i + 1]
                mp = mx(mp, _qk_chunk(q_n, k_ref, s_refs[(i + 1) % 2], cols_n, c, kc))
        outs.append(r[:, :LANES] * (1.0 / r[:, LANES:]))
    return outs


def _diff_attn_kernel(*refs, lam_init, n_self, has_ctx, heads, kc, sub):
    if has_ctx:
        (lamp_ref, subg_ref, q_ref, k_ref, v_ref, gate_ref, ck_ref, cv_ref, o_ref,
         s1_ref, s2_ref, kc_ref, vc_ref) = refs

        @pl.when(pl.program_id(2) == 0)
        def _():
            kc_ref[0:n_self, :] = k_ref[...]
            kc_ref[n_self:, :] = ck_ref[...].astype(BF16)
            vc_ref[0:n_self, :LANES] = v_ref[...]
            vc_ref[n_self:, :LANES] = cv_ref[...].astype(BF16)
            vc_ref[:, LANES:] = jnp.ones((vc_ref.shape[0], LANES), BF16)

        k_ref = kc_ref
    else:
        (lamp_ref, subg_ref, q_ref, k_ref, v_ref, gate_ref, o_ref, s1_ref, s2_ref) = refs
    n_keys = k_ref.shape[0]
    tq = q_ref.shape[0] // sub
    lp = lamp_ref[...]
    lam = (jnp.exp(jnp.sum(lp[0:1] * lp[1:2], axis=-1, keepdims=True))
           - jnp.exp(jnp.sum(lp[2:3] * lp[3:4], axis=-1, keepdims=True)) + lam_init)
    lane = lax.broadcasted_iota(jnp.int32, (tq, LANES), 1)
    zero = jnp.zeros((tq, LANES), BF16)
    streams, where = [], []
    for hd in range(heads):
        cols = slice(hd * LANES, (hd + 1) * LANES)
        if has_ctx:
            vaug = vc_ref[...]
        else:
            vaug = jnp.concatenate([v_ref[:, cols], jnp.ones((n_keys, LANES), BF16)], axis=1)
        for sb in range(sub):
            rows = slice(sb * tq, (sb + 1) * tq)
            q = q_ref[rows, cols]
            streams.append((jnp.where(lane < DIFF_HEAD_DIM, q, zero), cols, vaug))
            streams.append((jnp.where(lane >= DIFF_HEAD_DIM, q, zero), cols, vaug))
            where.append((rows, cols))
    outs = _pipelined_attention(streams, k_ref, (s1_ref, s2_ref), n_keys, kc)
    for n, (rows, cols) in enumerate(where):
        o = outs[2 * n] - lam * outs[2 * n + 1]
        y = o * lax.rsqrt(jnp.mean(o * o, axis=-1, keepdims=True) + EPS) * subg_ref[...]
        y = y * (1.0 - lam_init)
        o_ref[rows, cols] = (y * _silu(gate_ref[rows, cols].astype(F32))).astype(BF16)


def _diff_attention(p3, lamp, subg3, l, *, tq, heads, ctx=None, sub=1):
    b, s, _ = p3.shape
    w = heads * LANES
    cq, ck, cv, cg = (_OFF[n] // w for n in ("dq", "dk", "dv", "d_gate"))
    lam_init = 0.8 - 0.6 * math.exp(-0.3 * l)
    in_specs = [pl.BlockSpec((None, 4, DIFF_HEAD_DIM), lambda bi, h, qi: (l, 0, 0)),
                pl.BlockSpec((None, 1, LANES), lambda bi, h, qi: (l, 0, 0)),
                pl.BlockSpec((None, tq, w), lambda bi, h, qi: (bi, qi, cq + h)),
                pl.BlockSpec((None, s, w), lambda bi, h, qi: (bi, 0, ck + h)),
                pl.BlockSpec((None, s, w), lambda bi, h, qi: (bi, 0, cv + h)),
                pl.BlockSpec((None, tq, w), lambda bi, h, qi: (bi, qi, cg + h))]
    args = [lamp, subg3, p3, p3, p3, p3]
    n_keys = s
    scratch = []
    if ctx is not None:
        ck4, cv4 = ctx
        n_ctx = ck4.shape[2]
        n_keys = s + n_ctx
        for arr in (ck4, cv4):
            in_specs.append(pl.BlockSpec((None, None, n_ctx, w), lambda bi, h, qi: (bi, l, 0, h)))
            args.append(arr)
        scratch = [pltpu.VMEM((n_keys, w), BF16), pltpu.VMEM((n_keys, 2 * w), BF16)]
    kc = min(n_keys, 512)
    scratch = [pltpu.VMEM((tq // sub, n_keys), F32), pltpu.VMEM((tq // sub, n_keys), F32)] + scratch
    return pl.pallas_call(
        functools.partial(_diff_attn_kernel, lam_init=lam_init, n_self=s, has_ctx=ctx is not None,
                          heads=heads, kc=kc, sub=sub),
        grid=(b, DIFF_HEADS // heads, s // tq),
        in_specs=in_specs,
        out_specs=pl.BlockSpec((None, tq, w), lambda bi, h, qi: (bi, qi, h)),
        out_shape=jax.ShapeDtypeStruct((b, s, DIFF_HEADS * LANES), BF16),
        scratch_shapes=scratch,
        compiler_params=_params(("arbitrary", "arbitrary", "arbitrary")),
        name="diff_attn_latent" if ctx is not None else "diff_attn_prompt",
    )(*args)


def _softmax_rows_log2(s):
    m = jnp.max(s, axis=-1, keepdims=True)
    e = jnp.exp2(s - m)
    return e * (1.0 / jnp.sum(e, axis=-1, keepdims=True))


def _ctx_attn_kernel(q_ref, k_ref, v_ref, gate_ref, o_ref):
    dn = (((1,), (1,)), ((), ()))
    for hp in range(NA_HEADS // 2):
        cols = slice(hp * LANES, (hp + 1) * LANES)
        q, k, v = q_ref[:, cols], k_ref[:, cols], v_ref[:, cols]
        lane = lax.broadcasted_iota(jnp.int32, q.shape, 1)
        zero = jnp.zeros_like(q)
        outs = []
        for hh in range(2):
            sel = (lane < NA_HEAD_DIM) if hh == 0 else (lane >= NA_HEAD_DIM)
            qh = jnp.where(sel, q, zero)
            p = _softmax_rows_log2(lax.dot_general(qh, k, dn, preferred_element_type=F32)).astype(BF16)
            outs.append(jnp.dot(p, v, preferred_element_type=F32))
        o = jnp.where(lane < NA_HEAD_DIM, outs[0], outs[1])
        o_ref[:, cols] = (o * _silu(gate_ref[:, cols].astype(F32))).astype(BF16)


def _context_attention(p3):
    b, s, _ = p3.shape
    w = NA_HEADS * NA_HEAD_DIM
    cq, ck, cv, cg = (_OFF[n] // w for n in ("nq", "nk", "nv", "n_gate"))
    spec = lambda c: pl.BlockSpec((None, s, w), lambda bi: (bi, 0, c))
    return pl.pallas_call(
        _ctx_attn_kernel,
        grid=(b,),
        in_specs=[spec(cq), spec(ck), spec(cv), spec(cg)],
        out_specs=pl.BlockSpec((None, s, w), lambda bi: (bi, 0, 0)),
        out_shape=jax.ShapeDtypeStruct((b, s, w), BF16),
        compiler_params=_params(("arbitrary",)),
        name="ctx_attn_prompt",
    )(p3, p3, p3, p3)


def _na_attn_kernel(q_ref, k_ref, v_ref, gate_ref, ck_ref, cv_ref, bias_ref, o_ref,
                    sctx_ref, pctx_ref, sl_ref, vaug_ref, acc_ref, *, rows_per_step):
    rb = pl.program_id(2)
    n_loc = NA_WIN_R * GRID_W
    ones = jnp.ones((n_loc, LANES), BF16)

    @pl.when(rb == 0)
    def _():
        vaug_ref[:, :LANES] = v_ref[...]
        vaug_ref[:, LANES:] = jnp.ones((vaug_ref.shape[0], LANES), BF16)

    ckb = ck_ref[...].astype(BF16)
    cvaug = jnp.concatenate([cv_ref[...].astype(BF16), ones], axis=1)
    dn = (((1,), (1,)), ((), ()))
    q_all = q_ref[...]
    lane_all = lax.broadcasted_iota(jnp.int32, q_all.shape, 1)
    zero_all = jnp.zeros_like(q_all)
    for hh in range(2):
        sel = (lane_all < NA_HEAD_DIM) if hh == 0 else (lane_all >= NA_HEAD_DIM)
        sctx_ref[hh] = lax.dot_general(jnp.where(sel, q_all, zero_all), ckb, dn, preferred_element_type=F32)
    lane = lax.broadcasted_iota(jnp.int32, (GRID_W, LANES), 1)

    def ctx_scores(rl):
        rows = slice(rl * GRID_W, (rl + 1) * GRID_W)
        return jnp.concatenate([sctx_ref[0, rows, :], sctx_ref[1, rows, :]], axis=0)

    def score_stage(rl):
        r = rb * rows_per_step + rl
        r0 = jnp.clip(r - NA_WIN_R // 2, 0, GRID_H - NA_WIN_R)
        start = r0 - r + (NA_WIN_R - 1)
        koff = pl.multiple_of(r0 * GRID_W, GRID_W)
        q = q_ref[rl * GRID_W:(rl + 1) * GRID_W, :]
        zero = jnp.zeros_like(q)
        qs = jnp.concatenate([jnp.where(lane < NA_HEAD_DIM, q, zero),
                              jnp.where(lane >= NA_HEAD_DIM, q, zero)], axis=0)
        bias = jnp.concatenate([bias_ref[start, 0], bias_ref[start, 1]], axis=0)
        s = lax.dot_general(qs, k_ref[pl.ds(koff, n_loc), :], dn, preferred_element_type=F32) + bias
        sl_ref[rl % 2] = s
        mp = jnp.maximum(_lane_tile_reduce(jnp.maximum, s), _lane_tile_reduce(jnp.maximum, ctx_scores(rl)))
        return jnp.max(mp, axis=-1, keepdims=True), koff

    def value_stage(rl, m, koff):
        rows = slice(rl * GRID_W, (rl + 1) * GRID_W)
        e_ctx = jnp.exp2(ctx_scores(rl) - m).astype(BF16)
        pctx_ref[0, rows, :] = e_ctx[:GRID_W]
        pctx_ref[1, rows, :] = e_ctx[GRID_W:]
        e_loc = jnp.exp2(sl_ref[rl % 2] - m).astype(BF16)
        res = jnp.dot(e_loc, vaug_ref[pl.ds(koff, n_loc), :], preferred_element_type=F32)
        acc_ref[0, rows, :] = res[:GRID_W]
        acc_ref[1, rows, :] = res[GRID_W:]

    state = score_stage(0)
    for rl in range(rows_per_step):
        nxt = score_stage(rl + 1) if rl + 1 < rows_per_step else None
        value_stage(rl, *state)
        state = nxt
    outs = []
    for hh in range(2):
        tot = acc_ref[hh] + jnp.dot(pctx_ref[hh], cvaug, preferred_element_type=F32)
        outs.append(tot[:, :LANES] * (1.0 / tot[:, LANES:]))
    o = jnp.where(lane_all < NA_HEAD_DIM, outs[0], outs[1])
    o_ref[...] = (o * _silu(gate_ref[...].astype(F32))).astype(BF16)


def _na_bias_tables(rpb):
    c = np.arange(GRID_W)
    c0 = np.clip(c - NA_WIN_C // 2, 0, GRID_W - NA_WIN_C)
    in_win = (c[None, :] >= c0[:, None]) & (c[None, :] < c0[:, None] + NA_WIN_C)
    period = 2 * GRID_W - 1
    lead = rpb.shape[:-1]
    w = jnp.concatenate([rpb[..., NA_WIN_C - 1:], jnp.zeros(lead + (period - (2 * NA_WIN_C - 1),), rpb.dtype),
                         rpb[..., :NA_WIN_C - 1]], axis=-1)
    flat = jnp.tile(w, (1,) * len(lead) + (GRID_W,))[..., :GRID_W * (period - 1)]
    toe = flat.reshape(lead + (GRID_W, period - 1))[..., :GRID_W]
    toe = jnp.where(in_win, toe.astype(F32) * LOG2E, -1e30)
    tabs = jnp.stack([toe[:, :, s:s + NA_WIN_R] for s in range(NA_WIN_R)], axis=1)
    tabs = jnp.transpose(tabs, (0, 1, 2, 4, 3, 5))
    return tabs.reshape(DEPTH, NA_WIN_R, NA_HEADS, GRID_W, NA_WIN_R * GRID_W)


def _neighbourhood_attention(p3, ck4, cv4, bias, l, *, rows_per_step=8):
    b, s, _ = p3.shape
    cq, ck, cv, cg = (_OFF[n] // LANES for n in ("nq", "nk", "nv", "n_gate"))
    n_ctx = ck4.shape[2]
    tq = rows_per_step * GRID_W
    n_loc = NA_WIN_R * GRID_W
    return pl.pallas_call(
        functools.partial(_na_attn_kernel, rows_per_step=rows_per_step),
        grid=(b, NA_HEADS // 2, GRID_H // rows_per_step),
        in_specs=[pl.BlockSpec((None, tq, LANES), lambda bi, hp, rb: (bi, rb, cq + hp)),
                  pl.BlockSpec((None, s, LANES), lambda bi, hp, rb: (bi, 0, ck + hp)),
                  pl.BlockSpec((None, s, LANES), lambda bi, hp, rb: (bi, 0, cv + hp)),
                  pl.BlockSpec((None, tq, LANES), lambda bi, hp, rb: (bi, rb, cg + hp)),
                  pl.BlockSpec((None, None, n_ctx, LANES), lambda bi, hp, rb: (bi, l, 0, hp)),
                  pl.BlockSpec((None, None, n_ctx, LANES), lambda bi, hp, rb: (bi, l, 0, hp)),
                  pl.BlockSpec((None, NA_WIN_R, 2, GRID_W, n_loc), lambda bi, hp, rb: (l, 0, hp, 0, 0))],
        out_specs=pl.BlockSpec((None, tq, LANES), lambda bi, hp, rb: (bi, rb, hp)),
        out_shape=jax.ShapeDtypeStruct((b, s, NA_HEADS * NA_HEAD_DIM), BF16),
        scratch_shapes=[pltpu.VMEM((2, tq, n_ctx), F32), pltpu.VMEM((2, tq, n_ctx), BF16),
                        pltpu.VMEM((2, 2 * GRID_W, n_loc), F32), pltpu.VMEM((s, 2 * LANES), BF16),
                        pltpu.VMEM((2, tq, 2 * LANES), F32)],
        compiler_params=_params(("arbitrary", "arbitrary", "arbitrary")),
        name="na_attn_latent",
    )(p3, p3, p3, p3, ck4, cv4, bias)


def _dft_tables():
    gd = FNET_GROUP_DIM
    n = np.arange(gd)
    ang = 2.0 * np.pi * ((n[:, None] * n[None, :]) % gd) / gd
    fc = np.concatenate([np.cos(ang), -np.sin(ang)], axis=1) / 16.0
    fl = np.concatenate([np.cos(ang), np.sin(ang)], axis=1) / 16.0
    npos = GRID_H * GRID_W
    k1 = np.arange(64)[None, :, None]
    col = np.arange(64)[:, None, None]
    row = np.arange(64)[None, None, :]
    a1 = 2.0 * np.pi * ((k1 * (64 * row + col)) % npos) / npos
    t1 = np.concatenate([np.cos(a1), np.sin(a1)], axis=2) / 8.0
    k = np.arange(64)
    a2 = 2.0 * np.pi * ((k[:, None] * k[None, :]) % 64) / 64.0
    t2 = np.concatenate([np.cos(a2), np.sin(a2)], axis=1) / 8.0
    cast = lambda a: jnp.asarray(a.astype(np.float32)).astype(BF16)
    return cast(fc), cast(fl), cast(t1), cast(t2)


def _fft_prompt_kernel(h_ref, fg_ref, fc_ref, fl_ref, o_ref):
    gd = FNET_GROUP_DIM
    for g in range(FNET_GROUPS):
        cols = slice(g * gd, (g + 1) * gd)
        z = jnp.dot(h_ref[:, cols], fc_ref[...], preferred_element_type=F32)
        stack = jnp.concatenate([z[:, :gd], z[:, gd:]], axis=0).astype(BF16)
        y = jnp.dot(fl_ref[...], stack, preferred_element_type=F32)
        o_ref[:, cols] = (y * _silu(fg_ref[:, cols].astype(F32))).astype(BF16)


def _fourier_prompt(h2d, p2d, fc, fl, b, s):
    gd = FNET_GROUP_DIM
    cf = _OFF["f_gate"] // D_MODEL
    return pl.pallas_call(
        _fft_prompt_kernel,
        grid=(b,),
        in_specs=[pl.BlockSpec((s, D_MODEL), lambda bi: (bi, 0)),
                  pl.BlockSpec((s, D_MODEL), lambda bi: (bi, cf)),
                  pl.BlockSpec((gd, 2 * gd), lambda bi: (0, 0)),
                  pl.BlockSpec((s, 2 * s), lambda bi: (0, 0))],
        out_specs=pl.BlockSpec((s, D_MODEL), lambda bi: (bi, 0)),
        out_shape=jax.ShapeDtypeStruct((b * s, D_MODEL), BF16),
        compiler_params=_params(("arbitrary",)),
        name="fourier_prompt",
    )(h2d, p2d, fc, fl)


def _fft_latent_kernel(h_ref, fg_ref, fc_ref, t1_ref, t2_ref, o_ref, z_ref, a_ref, y_ref):
    gd = FNET_GROUP_DIM
    n = GRID_H * GRID_W
    rows = 512
    nt = 2 * gd // LANES
    for c in range(n // rows):
        sl = slice(c * rows, (c + 1) * rows)
        z = jnp.dot(h_ref[sl, :], fc_ref[...], preferred_element_type=F32)
        for t in range(nt):
            z_ref[t, sl, :] = z[:, t * LANES:(t + 1) * LANES]

    def stage1(col, carry):
        tiles = [z_ref[t, pl.ds(col, GRID_H, stride=GRID_W), :] for t in range(nt)]
        zr = jnp.concatenate(tiles[:nt // 2], axis=1)
        zi = jnp.concatenate(tiles[nt // 2:], axis=1)
        stack = jnp.concatenate([jnp.concatenate([zr, zi], axis=1),
                                 jnp.concatenate([zi, -zr], axis=1)], axis=0).astype(BF16)
        a = jnp.dot(t1_ref[col], stack, preferred_element_type=F32)
        off = pl.multiple_of(col * GRID_H, GRID_H)
        for t in range(nt):
            a_ref[t, pl.ds(off, GRID_H), :] = a[:, t * LANES:(t + 1) * LANES]
        return carry

    lax.fori_loop(0, GRID_W, stage1, 0, unroll=8)

    def stage2(k1, carry):
        tiles = [a_ref[t, pl.ds(k1, GRID_W, stride=GRID_H), :] for t in range(nt)]
        stack = jnp.concatenate([jnp.concatenate(tiles[:nt // 2], axis=1),
                                 jnp.concatenate(tiles[nt // 2:], axis=1)], axis=0).astype(BF16)
        y = jnp.dot(t2_ref[...], stack, preferred_element_type=F32)
        for t in range(nt // 2):
            y_ref[t, pl.ds(k1, GRID_H, stride=GRID_W), :] = y[:, t * LANES:(t + 1) * LANES]
        return carry

    lax.fori_loop(0, GRID_H, stage2, 0, unroll=8)
    for c in range(n // rows):
        sl = slice(c * rows, (c + 1) * rows)
        y = jnp.concatenate([y_ref[t, sl, :] for t in range(nt // 2)], axis=1)
        o_ref[sl, :] = (y * _silu(fg_ref[sl, :].astype(F32))).astype(BF16)


def _fourier_latent(h2d, p2d, fc, t1, t2, b):
    gd = FNET_GROUP_DIM
    n = GRID_H * GRID_W
    cf = _OFF["f_gate"] // gd
    return pl.pallas_call(
        _fft_latent_kernel,
        grid=(b, FNET_GROUPS),
        in_specs=[pl.BlockSpec((n, gd), lambda bi, g: (bi, g)),
                  pl.BlockSpec((n, gd), lambda bi, g: (bi, cf + g)),
                  pl.BlockSpec((gd, 2 * gd), lambda bi, g: (0, 0)),
                  pl.BlockSpec((GRID_W, GRID_H, 2 * GRID_H), lambda bi, g: (0, 0, 0)),
                  pl.BlockSpec((GRID_H, 2 * GRID_W), lambda bi, g: (0, 0))],
        out_specs=pl.BlockSpec((n, gd), lambda bi, g: (bi, g)),
        out_shape=jax.ShapeDtypeStruct((b * n, D_MODEL), BF16),
        scratch_shapes=[pltpu.VMEM((2 * gd // LANES, n, LANES), F32), pltpu.VMEM((2 * gd // LANES, n, LANES), F32),
                        pltpu.VMEM((gd // LANES, n, LANES), F32)],
        compiler_params=_params(("arbitrary", "arbitrary")),
        name="fourier_latent",
    )(h2d, p2d, fc, t1, t2)


def _merge_kernel(x_ref, mod_ref, mg_ref, u_ref, v_ref, sgate_ref, yd_ref, yn_ref, yf_ref,
                  gv_ref, sw_ref, sb_ref, psg_ref, pd_ref, pn_ref, pf_ref, wo_ref, gpost_ref,
                  o_ref, ysg_ref, *, latent, tm, tiles_per_batch):
    i = pl.program_id(0)
    lane = lax.broadcasted_iota(jnp.int32, (CHUNK, LANES), 1)
    for c in range(tm // CHUNK):
        sl = slice(c * CHUNK, (c + 1) * CHUNK)
        v = v_ref[sl, :].astype(F32)
        vn = (v * lax.rsqrt(jnp.mean(v * v, axis=-1, keepdims=True) + EPS) * gv_ref[...]).astype(BF16)
        parts = []
        for pr in range(SG_GROUPS // 2):
            vp = vn[:, pr * LANES:(pr + 1) * LANES]
            s0 = jnp.dot(sw_ref[2 * pr], vp, preferred_element_type=F32)
            s1 = jnp.dot(sw_ref[2 * pr + 1], vp, preferred_element_type=F32)
            parts.append(jnp.where(lane < SG_WIDTH // SG_GROUPS, s0, s1))
        s = jnp.concatenate(parts, axis=1) + sb_ref[...]
        y = u_ref[sl, :].astype(F32) * s * _silu(sgate_ref[sl, :].astype(F32))
        ysg_ref[sl, :] = y.astype(BF16)

    def gate(n):
        return jax.nn.sigmoid(mg_ref[:, n * D_MODEL:(n + 1) * D_MODEL].astype(F32))

    mixed = gate(0) * jnp.dot(ysg_ref[...], psg_ref[...], preferred_element_type=F32)
    mixed = mixed + gate(1) * jnp.dot(yd_ref[...], pd_ref[...], preferred_element_type=F32)
    mixed = mixed + gate(2) * jnp.dot(yn_ref[...], pn_ref[...], preferred_element_type=F32)
    mixed = mixed + gate(3) * jnp.dot(yf_ref[...], pf_ref[...], preferred_element_type=F32)
    out = jnp.dot(mixed.astype(BF16), wo_ref[...], preferred_element_type=F32)
    out = out * lax.rsqrt(jnp.mean(out * out, axis=-1, keepdims=True) + EPS) * gpost_ref[...]
    row = (1 + i // tiles_per_batch) if latent else 0
    g = mod_ref[pl.ds(row, 1), :][:, 2 * D_MODEL:]
    o_ref[...] = x_ref[...] + g * out


def _merge(x2d, mod_all, p2d, yd, yn, yf, w, l, *, latent, seq, tm=512):
    t = x2d.shape[0]
    cu, cv, cs = (_OFF[n] // SG_WIDTH for n in ("sg_u", "sg_v", "sg_gate"))
    tok = lambda width, c=0: pl.BlockSpec((tm, width), lambda i: (i, c))
    wspec = lambda *shape: pl.BlockSpec((None,) + shape, lambda i: (l,) + (0,) * len(shape))
    tiles_per_batch = seq // tm if latent else 1
    return pl.pallas_call(
        functools.partial(_merge_kernel, latent=latent, tm=tm, tiles_per_batch=tiles_per_batch),
        grid=(t // tm,),
        in_specs=[tok(D_MODEL), wspec(8, 3 * D_MODEL), tok(N_BRANCH * D_MODEL, 0),
                  tok(SG_WIDTH, cu), tok(SG_WIDTH, cv), tok(SG_WIDTH, cs),
                  tok(512), tok(512), tok(D_MODEL),
                  wspec(1, SG_WIDTH), wspec(SG_GROUPS, CHUNK, CHUNK), wspec(CHUNK, SG_WIDTH),
                  wspec(SG_WIDTH, D_MODEL), wspec(512, D_MODEL), wspec(512, D_MODEL),
                  wspec(D_MODEL, D_MODEL), wspec(D_MODEL, D_MODEL), wspec(1, D_MODEL)],
        out_specs=tok(D_MODEL),
        out_shape=jax.ShapeDtypeStruct((t, D_MODEL), F32),
        scratch_shapes=[pltpu.VMEM((tm, SG_WIDTH), BF16)],
        compiler_params=_params(("arbitrary",)),
        name="merge_latent" if latent else "merge_prompt",
    )(x2d, mod_all, p2d, p2d, p2d, p2d, yd, yn, yf,
      w["sg_norm_g"], w["sg_w"], w["sg_b"], w["p_sg"], w["p_diff"], w["p_na"], w["p_fnet"], w["w_out"],
      w["g_post"])


def _rope_tables(n_tokens):
    nf = DIFF_HEAD_DIM // 4
    t = np.arange(n_tokens)
    inv = ROPE_BASE ** (-np.arange(nf, dtype=np.float64) / nf)
    ang_r = (t // GRID_W).astype(np.float64)[:, None] * inv[None, :]
    ang_c = (t % GRID_W).astype(np.float64)[:, None] * inv[None, :]
    z = np.zeros_like(ang_r)
    cos64 = np.concatenate([np.cos(ang_r), np.cos(ang_r), np.cos(ang_c), np.cos(ang_c)], axis=1)
    sina64 = np.concatenate([-np.sin(ang_r), z, -np.sin(ang_c), z], axis=1)
    sinb64 = np.concatenate([z, np.sin(ang_r), z, np.sin(ang_c)], axis=1)
    two = lambda a: jnp.asarray(np.concatenate([a, a], axis=1).astype(np.float32))
    return two(cos64), two(sina64), two(sinb64)


def kernel(x_prompt, x_sample, cache_diff_k, cache_diff_v, cache_na_k, cache_na_v, c, c_ctx,
           w_mod, b_mod, g_pre, g_post, w_in, sg_norm_g, sg_w, sg_b,
           diff_lam_q1, diff_lam_k1, diff_lam_q2, diff_lam_k2, diff_subln_g, na_rpb,
           w_proj_sg, w_proj_diff, w_proj_na, w_proj_fnet, w_out):
    bp, sp, _ = x_prompt.shape
    bs, ss, _ = x_sample.shape
    n_ctx = cache_diff_k.shape[2]

    cond8 = jnp.zeros((8, D_MODEL), F32).at[0].set(c_ctx).at[1:1 + bs].set(c)
    mod_all = _modulation(cond8, w_mod, b_mod)

    w_in_b = jnp.concatenate(
        [w_in[:, :, _REF_OFF[n][0]:_REF_OFF[n][0] + _REF_OFF[n][1]] for n in _MY_ORDER], axis=-1).astype(BF16)
    weights = {
        "sg_norm_g": sg_norm_g.reshape(DEPTH, 1, SG_WIDTH),
        "sg_w": sg_w.astype(BF16),
        "sg_b": jnp.repeat(jnp.transpose(sg_b, (0, 2, 1)), SG_WIDTH // SG_GROUPS, axis=2),
        "p_sg": w_proj_sg.astype(BF16), "p_diff": w_proj_diff.astype(BF16),
        "p_na": w_proj_na.astype(BF16), "p_fnet": w_proj_fnet.astype(BF16),
        "w_out": w_out.astype(BF16), "g_post": g_post.reshape(DEPTH, 1, D_MODEL),
    }
    g_pre3 = g_pre.reshape(DEPTH, 1, D_MODEL)
    lamp = jnp.stack([diff_lam_q1, diff_lam_k1, diff_lam_q2, diff_lam_k2], axis=1)
    subg3 = diff_subln_g.reshape(DEPTH, 1, 2 * DIFF_HEAD_DIM)
    rope_tabs = _rope_tables(ss)
    fc, fl, t1, t2 = _dft_tables()
    bias = _na_bias_tables(na_rpb)
    ckd = cache_diff_k.reshape(bs, DEPTH, n_ctx, DIFF_HEADS * 2 * DIFF_HEAD_DIM)
    cvd = cache_diff_v.reshape(bs, DEPTH, n_ctx, DIFF_HEADS * 2 * DIFF_HEAD_DIM)
    ckn = cache_na_k.reshape(bs, DEPTH, n_ctx, NA_HEADS * NA_HEAD_DIM)
    cvn = cache_na_v.reshape(bs, DEPTH, n_ctx, NA_HEADS * NA_HEAD_DIM)

    xp = x_prompt.reshape(bp * sp, D_MODEL)
    xs = x_sample.reshape(bs * ss, D_MODEL)
    caches = None
    for l in range(DEPTH):
        pp, hp, *caches = _inproj(xp, mod_all, g_pre3, w_in_b, l, latent=False, tm=4 * sp, seq=sp, n_col_tiles=7,
                                  cache_bufs=caches)
        pp3 = pp.reshape(bp, sp, IN_WIDTH)
        yd = _diff_attention(pp3, lamp, subg3, l, tq=sp, heads=DIFF_HEADS).reshape(bp * sp, -1)
        yn = _context_attention(pp3).reshape(bp * sp, -1)
        yf = _fourier_prompt(hp, pp, fc, fl, bp, sp)
        xp = _merge(xp, mod_all, pp, yd, yn, yf, weights, l, latent=False, seq=sp)
        ps, hs = _inproj(xs, mod_all, g_pre3, w_in_b, l, latent=True, tm=1024, seq=ss, n_col_tiles=3,
                         rope_tabs=rope_tabs)
        ps3 = ps.reshape(bs, ss, IN_WIDTH)
        yd = _diff_attention(ps3, lamp, subg3, l, tq=1024, heads=1, sub=4, ctx=(ckd, cvd)).reshape(bs * ss, -1)
        yn = _neighbourhood_attention(ps3, ckn, cvn, bias, l).reshape(bs * ss, -1)
        yf = _fourier_latent(hs, ps, fc, t1, t2, bs)
        xs = _merge(xs, mod_all, ps, yd, yn, yf, weights, l, latent=True, seq=ss)

    dk5, dv5, nk5, nv5 = caches
    return (xp.reshape(bp, sp, D_MODEL), xs.reshape(bs, ss, D_MODEL),
            dk5.reshape(bp, DEPTH, sp, DIFF_HEADS, 2 * DIFF_HEAD_DIM),
            dv5.reshape(bp, DEPTH, sp, DIFF_HEADS, 2 * DIFF_HEAD_DIM),
            nk5.reshape(bp, DEPTH, sp, NA_HEADS, NA_HEAD_DIM),
            nv5.reshape(bp, DEPTH, sp, NA_HEADS, NA_HEAD_DIM))
```

```python
import functools
import math

import numpy as np
import jax
import jax.numpy as jnp
from jax import lax
from jax.experimental import pallas as pl
from jax.experimental.pallas import tpu as pltpu

F32 = jnp.float32
BF16 = jnp.bfloat16

D_MODEL = 1024
DEPTH = 4
GRID_W = 64
GRID_H = 64
EPS = 1e-6
ROPE_BASE = 10000.0
CHUNK = 128
SG_WIDTH = 512
SG_GROUPS = 8
DIFF_HEADS = 4
DIFF_HEAD_DIM = 64
NA_HEADS = 8
NA_HEAD_DIM = 64
NA_WIN_R = 8
NA_WIN_C = 16
FNET_GROUPS = 4
FNET_GROUP_DIM = 256
N_BRANCH = 4
LANES = 128
VMEM_LIMIT = 56 * 1024 * 1024
LOG2E = math.log2(math.e)

_REF_SEGMENTS = (("sg_u", 512), ("sg_v", 512), ("sg_gate", 512),
                 ("dq", 512), ("dk", 512), ("dv", 512), ("d_gate", 512),
                 ("nq", 512), ("nk", 512), ("nv", 512), ("n_gate", 512),
                 ("f_gate", 1024), ("merge", 4096))
_MY_ORDER = ("merge", "f_gate", "sg_u", "sg_v", "sg_gate", "dq", "dk", "dv", "d_gate",
             "nq", "nk", "nv", "n_gate")
UNIT = 512


def _layout():
    ref_off, o = {}, 0
    for name, width in _REF_SEGMENTS:
        ref_off[name] = (o, width)
        o += width
    my_off, o, unit_names = {}, 0, []
    for name in _MY_ORDER:
        my_off[name] = o
        o += ref_off[name][1]
        unit_names += [name] * (ref_off[name][1] // UNIT)
    return ref_off, my_off, o, tuple(unit_names)


_REF_OFF, _OFF, IN_WIDTH, _UNIT_NAMES = _layout()


def _ref_unit(n):
    merge_units = _REF_OFF["merge"][1] // UNIT
    gate_units = _REF_OFF["f_gate"][1] // UNIT
    return jnp.where(n < merge_units, n + _REF_OFF["merge"][0] // UNIT,
                     jnp.where(n < merge_units + gate_units, n - merge_units + _REF_OFF["f_gate"][0] // UNIT,
                               n - merge_units - gate_units))


def _silu(x):
    return x * jax.nn.sigmoid(x)


def _params(sem):
    return pltpu.CompilerParams(dimension_semantics=sem, vmem_limit_bytes=VMEM_LIMIT)


def _mod_kernel(cond_ref, w_ref, b_ref, o_ref):
    s = _silu(cond_ref[...])
    o_ref[...] = jnp.dot(s, w_ref[...], precision=lax.Precision.HIGHEST,
                         preferred_element_type=F32) + b_ref[...]


def _modulation(cond8, w_mod, b_mod):
    b3 = b_mod.reshape(DEPTH, 1, 3 * D_MODEL)
    return pl.pallas_call(
        _mod_kernel,
        grid=(DEPTH, 3),
        in_specs=[pl.BlockSpec((8, D_MODEL), lambda l, j: (0, 0)),
                  pl.BlockSpec((None, D_MODEL, D_MODEL), lambda l, j: (l, 0, j)),
                  pl.BlockSpec((None, 1, D_MODEL), lambda l, j: (l, 0, j))],
        out_specs=pl.BlockSpec((None, 8, D_MODEL), lambda l, j: (l, 0, j)),
        out_shape=jax.ShapeDtypeStruct((DEPTH, 8, 3 * D_MODEL), F32),
        compiler_params=_params(("arbitrary", "arbitrary")),
        name="modulation",
    )(cond8, w_mod, b3)


def _rope(acc, cos_ref, sina_ref, sinb_ref):
    cos, sina, sinb = cos_ref[...], sina_ref[...], sinb_ref[...]
    outs = []
    for hd in range(acc.shape[1] // LANES):
        a = acc[:, hd * LANES:(hd + 1) * LANES]
        up = pltpu.roll(a, LANES - 16, 1)
        dn = pltpu.roll(a, 16, 1)
        outs.append(a * cos + up * sina + dn * sinb)
    return jnp.concatenate(outs, axis=1)


def _inproj_kernel(*refs, latent, tm, seq, tiles_per_batch, n_alias, n_col_tiles):
    units_per_tile = len(_UNIT_NAMES) // n_col_tiles
    (x_ref, mod_ref, gpre_ref) = refs[:3]
    w_refs = refs[3:3 + units_per_tile]
    rest = refs[3 + units_per_tile:]
    if latent:
        (cos_ref, sina_ref, sinb_ref, p_ref, h_ref) = rest
        cache_refs = {}
    else:
        (p_ref, h_ref, dk_ref, dv_ref, nk_ref, nv_ref) = rest[n_alias:]
        cache_refs = {"dk": dk_ref, "dv": dv_ref, "nk": nk_ref, "nv": nv_ref}
    i = pl.program_id(0)
    j = pl.program_id(1)

    @pl.when(j == 0)
    def _():
        x = x_ref[...]
        r = lax.rsqrt(jnp.mean(x * x, axis=-1, keepdims=True) + EPS)
        row = (1 + i // tiles_per_batch) if latent else 0
        m = mod_ref[pl.ds(row, 1), :]
        shift = m[:, :D_MODEL]
        scale = m[:, D_MODEL:2 * D_MODEL]
        h_ref[...] = ((x * r * gpre_ref[...]) * (1.0 + scale) + shift).astype(BF16)

    def unit(u, name):
        cols = slice(u * UNIT, (u + 1) * UNIT)
        acc = jnp.dot(h_ref[...], w_refs[u][...], preferred_element_type=F32)
        if name in ("dk", "dv") and not latent:
            for hd in range(DIFF_HEADS):
                cache_refs[name][:, :, hd, :] = acc[:, hd * LANES:(hd + 1) * LANES].reshape(tm // seq, seq, LANES)
        elif name in cache_refs:
            cache_refs[name][...] = acc.reshape(tm // seq, seq, UNIT)
        if name in ("dq", "nq"):
            acc = acc * (DIFF_HEAD_DIM ** -0.5 * LOG2E)
        if latent and name in ("dq", "dk"):
            acc = _rope(acc, cos_ref, sina_ref, sinb_ref)
        p_ref[:, cols] = acc.astype(BF16)

    for jt in range(n_col_tiles):
        @pl.when(j == jt)
        def _(jt=jt):
            for u in range(units_per_tile):
                unit(u, _UNIT_NAMES[jt * units_per_tile + u])


def _inproj(x2d, mod_all, g_pre3, w_in_b, l, *, latent, tm, seq, n_col_tiles, rope_tabs=None, cache_bufs=None):
    t = x2d.shape[0]
    n_i = t // tm
    tn = IN_WIDTH // n_col_tiles
    units_per_tile = tn // UNIT
    in_specs = [pl.BlockSpec((tm, D_MODEL), lambda i, j: (i, 0)),
                pl.BlockSpec((None, 8, 3 * D_MODEL), lambda i, j: (l, 0, 0)),
                pl.BlockSpec((None, 1, D_MODEL), lambda i, j: (l, 0, 0))]
    args = [x2d, mod_all, g_pre3]
    for u in range(units_per_tile):
        in_specs.append(pl.BlockSpec((None, D_MODEL, UNIT),
                                     lambda i, j, u=u: (l, 0, _ref_unit(j * units_per_tile + u))))
        args.append(w_in_b)
    out_specs = [pl.BlockSpec((tm, tn), lambda i, j: (i, j)),
                 pl.BlockSpec((tm, D_MODEL), lambda i, j: (i, 0))]
    out_shape = [jax.ShapeDtypeStruct((t, IN_WIDTH), BF16),
                 jax.ShapeDtypeStruct((t, D_MODEL), BF16)]
    tiles_per_batch = 1
    aliases = {}
    n_alias = 0
    if latent:
        tiles_per_batch = seq // tm
        for tab in rope_tabs:
            in_specs.append(pl.BlockSpec((tm, LANES), lambda i, j: (i % tiles_per_batch, 0)))
            args.append(tab)
    else:
        nb = tm // seq
        if cache_bufs is not None:
            n_alias = len(cache_bufs)
            for k, buf in enumerate(cache_bufs):
                in_specs.append(pl.BlockSpec(memory_space=pl.ANY))
                args.append(buf)
                aliases[3 + units_per_tile + k] = 2 + k
        for _ in range(2):
            out_specs.append(pl.BlockSpec((nb, None, seq, DIFF_HEADS, LANES), lambda i, j: (i, l, 0, 0, 0)))
            out_shape.append(jax.ShapeDtypeStruct((t // seq, DEPTH, seq, DIFF_HEADS, LANES), F32))
        for _ in range(2):
            out_specs.append(pl.BlockSpec((nb, None, seq, UNIT), lambda i, j: (i, l, 0, 0)))
            out_shape.append(jax.ShapeDtypeStruct((t // seq, DEPTH, seq, UNIT), F32))
    return pl.pallas_call(
        functools.partial(_inproj_kernel, latent=latent, tm=tm, seq=seq, tiles_per_batch=tiles_per_batch,
                          n_alias=n_alias, n_col_tiles=n_col_tiles),
        grid=(n_i, n_col_tiles),
        in_specs=in_specs,
        out_specs=out_specs,
        out_shape=out_shape,
        input_output_aliases=aliases,
        compiler_params=_params(("arbitrary", "arbitrary")),
        name="inproj_latent" if latent else "inproj_prompt",
    )(*args)


def _lane_tile_reduce(op, x):
    parts = [x[:, t * LANES:(t + 1) * LANES] for t in range(x.shape[1] // LANES)]
    return functools.reduce(op, parts)


def _qk_chunk(q, k_ref, s_ref, kcols, c, kc):
    rows = slice(c * kc, (c + 1) * kc)
    s = lax.dot_general(q, k_ref[rows, kcols], (((1,), (1,)), ((), ())), preferred_element_type=F32)
    s_ref[:, rows] = s
    return _lane_tile_reduce(jnp.maximum, s)


def _pv_chunk(s_ref, m, vaug, c, kc):
    rows = slice(c * kc, (c + 1) * kc)
    e = jnp.exp2(s_ref[:, rows] - m).astype(BF16)
    return jnp.dot(e, vaug[rows], preferred_element_type=F32)


def _pipelined_attention(streams, k_ref, s_refs, n_keys, kc):
    nck = n_keys // kc
    acc = lambda a, b: b if a is None else a + b
    mx = lambda a, b: b if a is None else jnp.maximum(a, b)
    outs = []
    if nck == 1:
        for q, cols, vaug in streams:
            s = lax.dot_general(q, k_ref[:, cols], (((1,), (1,)), ((), ())), preferred_element_type=F32)
            e = jnp.exp2(s - jnp.max(s, axis=-1, keepdims=True)).astype(BF16)
            r = jnp.dot(e, vaug, preferred_element_type=F32)
            outs.append(r[:, :LANES] * (1.0 / r[:, LANES:]))
        return outs
    mp = None
    for c in range(nck):
        mp = mx(mp, _qk_chunk(streams[0][0], k_ref, s_refs[0], streams[0][1], c, kc))
    for i, (_, _, vaug) in enumerate(streams):
        m = jnp.max(mp, axis=-1, keepdims=True)
        r = mp = None
        for c in range(nck):
            r = acc(r, _pv_chunk(s_refs[i % 2], m, vaug, c, kc))
            if i + 1 < len(streams):
                q_n, cols_n, _ = streams[i + 1]
                mp = mx(mp, _qk_chunk(q_n, k_ref, s_refs[(i + 1) % 2], cols_n, c, kc))
        outs.append(r[:, :LANES] * (1.0 / r[:, LANES:]))
    return outs


def _diff_attn_kernel(*refs, lam_init, n_self, has_ctx, heads, kc, sub):
    if has_ctx:
        (lamp_ref, subg_ref, q_ref, k_ref, v_ref, gate_ref, ck_ref, cv_ref, o_ref,
         s1_ref, s2_ref, kc_ref, vc_ref) = refs

        @pl.when(pl.program_id(2) == 0)
        def _():
            kc_ref[0:n_self, :] = k_ref[...]
            kc_ref[n_self:, :] = ck_ref[...].astype(BF16)
            vc_ref[0:n_self, :LANES] = v_ref[...]
            vc_ref[n_self:, :LANES] = cv_ref[...].astype(BF16)
            vc_ref[:, LANES:] = jnp.ones((vc_ref.shape[0], LANES), BF16)

        k_ref = kc_ref
    else:
        (lamp_ref, subg_ref, q_ref, k_ref, v_ref, gate_ref, o_ref, s1_ref, s2_ref) = refs
    n_keys = k_ref.shape[0]
    tq = q_ref.shape[0] // sub
    lp = lamp_ref[...]
    lam = (jnp.exp(jnp.sum(lp[0:1] * lp[1:2], axis=-1, keepdims=True))
           - jnp.exp(jnp.sum(lp[2:3] * lp[3:4], axis=-1, keepdims=True)) + lam_init)
    lane = lax.broadcasted_iota(jnp.int32, (tq, LANES), 1)
    zero = jnp.zeros((tq, LANES), BF16)
    streams, where = [], []
    for hd in range(heads):
        cols = slice(hd * LANES, (hd + 1) * LANES)
        if has_ctx:
            vaug = vc_ref[...]
        else:
            vaug = jnp.concatenate([v_ref[:, cols], jnp.ones((n_keys, LANES), BF16)], axis=1)
        for sb in range(sub):
            rows = slice(sb * tq, (sb + 1) * tq)
            q = q_ref[rows, cols]
            streams.append((jnp.where(lane < DIFF_HEAD_DIM, q, zero), cols, vaug))
            streams.append((jnp.where(lane >= DIFF_HEAD_DIM, q, zero), cols, vaug))
            where.append((rows, cols))
    outs = _pipelined_attention(streams, k_ref, (s1_ref, s2_ref), n_keys, kc)
    for n, (rows, cols) in enumerate(where):
        o = outs[2 * n] - lam * outs[2 * n + 1]
        y = o * lax.rsqrt(jnp.mean(o * o, axis=-1, keepdims=True) + EPS) * subg_ref[...]
        y = y * (1.0 - lam_init)
        o_ref[rows, cols] = (y * _silu(gate_ref[rows, cols].astype(F32))).astype(BF16)


def _diff_attention(p3, lamp, subg3, l, *, tq, heads, ctx=None, sub=1):
    b, s, _ = p3.shape
    w = heads * LANES
    cq, ck, cv, cg = (_OFF[n] // w for n in ("dq", "dk", "dv", "d_gate"))
    lam_init = 0.8 - 0.6 * math.exp(-0.3 * l)
    in_specs = [pl.BlockSpec((None, 4, DIFF_HEAD_DIM), lambda bi, h, qi: (l, 0, 0)),
                pl.BlockSpec((None, 1, LANES), lambda bi, h, qi: (l, 0, 0)),
                pl.BlockSpec((None, tq, w), lambda bi, h, qi: (bi, qi, cq + h)),
                pl.BlockSpec((None, s, w), lambda bi, h, qi: (bi, 0, ck + h)),
                pl.BlockSpec((None, s, w), lambda bi, h, qi: (bi, 0, cv + h)),
                pl.BlockSpec((None, tq, w), lambda bi, h, qi: (bi, qi, cg + h))]
    args = [lamp, subg3, p3, p3, p3, p3]
    n_keys = s
    scratch = []
    if ctx is not None:
        ck4, cv4 = ctx
        n_ctx = ck4.shape[2]
        n_keys = s + n_ctx
        for arr in (ck4, cv4):
            in_specs.append(pl.BlockSpec((None, None, n_ctx, w), lambda bi, h, qi: (bi, l, 0, h)))
            args.append(arr)
        scratch = [pltpu.VMEM((n_keys, w), BF16), pltpu.VMEM((n_keys, 2 * w), BF16)]
    kc = min(n_keys, 512)
    scratch = [pltpu.VMEM((tq // sub, n_keys), F32), pltpu.VMEM((tq // sub, n_keys), F32)] + scratch
    return pl.pallas_call(
        functools.partial(_diff_attn_kernel, lam_init=lam_init, n_self=s, has_ctx=ctx is not None,
                          heads=heads, kc=kc, sub=sub),
        grid=(b, DIFF_HEADS // heads, s // tq),
        in_specs=in_specs,
        out_specs=pl.BlockSpec((None, tq, w), lambda bi, h, qi: (bi, qi, h)),
        out_shape=jax.ShapeDtypeStruct((b, s, DIFF_HEADS * LANES), BF16),
        scratch_shapes=scratch,
        compiler_params=_params(("arbitrary", "arbitrary", "arbitrary")),
        name="diff_attn_latent" if ctx is not None else "diff_attn_prompt",
    )(*args)


def _softmax_rows_log2(s):
    m = jnp.max(s, axis=-1, keepdims=True)
    e = jnp.exp2(s - m)
    return e * (1.0 / jnp.sum(e, axis=-1, keepdims=True))


def _ctx_attn_kernel(q_ref, k_ref, v_ref, gate_ref, o_ref):
    dn = (((1,), (1,)), ((), ()))
    for hp in range(NA_HEADS // 2):
        cols = slice(hp * LANES, (hp + 1) * LANES)
        q, k, v = q_ref[:, cols], k_ref[:, cols], v_ref[:, cols]
        lane = lax.broadcasted_iota(jnp.int32, q.shape, 1)
        zero = jnp.zeros_like(q)
        outs = []
        for hh in range(2):
            sel = (lane < NA_HEAD_DIM) if hh == 0 else (lane >= NA_HEAD_DIM)
            qh = jnp.where(sel, q, zero)
            p = _softmax_rows_log2(lax.dot_general(qh, k, dn, preferred_element_type=F32)).astype(BF16)
            outs.append(jnp.dot(p, v, preferred_element_type=F32))
        o = jnp.where(lane < NA_HEAD_DIM, outs[0], outs[1])
        o_ref[:, cols] = (o * _silu(gate_ref[:, cols].astype(F32))).astype(BF16)


def _context_attention(p3):
    b, s, _ = p3.shape
    w = NA_HEADS * NA_HEAD_DIM
    cq, ck, cv, cg = (_OFF[n] // w for n in ("nq", "nk", "nv", "n_gate"))
    spec = lambda c: pl.BlockSpec((None, s, w), lambda bi: (bi, 0, c))
    return pl.pallas_call(
        _ctx_attn_kernel,
        grid=(b,),
        in_specs=[spec(cq), spec(ck), spec(cv), spec(cg)],
        out_specs=pl.BlockSpec((None, s, w), lambda bi: (bi, 0, 0)),
        out_shape=jax.ShapeDtypeStruct((b, s, w), BF16),
        compiler_params=_params(("arbitrary",)),
        name="ctx_attn_prompt",
    )(p3, p3, p3, p3)


def _na_attn_kernel(q_ref, k_ref, v_ref, gate_ref, ck_ref, cv_ref, bias_ref, o_ref,
                    sctx_ref, pctx_ref, sl_ref, vaug_ref, acc_ref, *, rows_per_step):
    rb = pl.program_id(2)
    n_loc = NA_WIN_R * GRID_W
    ones = jnp.ones((n_loc, LANES), BF16)

    @pl.when(rb == 0)
    def _():
        vaug_ref[:, :LANES] = v_ref[...]
        vaug_ref[:, LANES:] = jnp.ones((vaug_ref.shape[0], LANES), BF16)

    ckb = ck_ref[...].astype(BF16)
    cvaug = jnp.concatenate([cv_ref[...].astype(BF16), ones], axis=1)
    dn = (((1,), (1,)), ((), ()))
    q_all = q_ref[...]
    lane_all = lax.broadcasted_iota(jnp.int32, q_all.shape, 1)
    zero_all = jnp.zeros_like(q_all)
    for hh in range(2):
        sel = (lane_all < NA_HEAD_DIM) if hh == 0 else (lane_all >= NA_HEAD_DIM)
        sctx_ref[hh] = lax.dot_general(jnp.where(sel, q_all, zero_all), ckb, dn, preferred_element_type=F32)
    lane = lax.broadcasted_iota(jnp.int32, (GRID_W, LANES), 1)

    def ctx_scores(rl):
        rows = slice(rl * GRID_W, (rl + 1) * GRID_W)
        return jnp.concatenate([sctx_ref[0, rows, :], sctx_ref[1, rows, :]], axis=0)

    def score_stage(rl):
        r = rb * rows_per_step + rl
        r0 = jnp.clip(r - NA_WIN_R // 2, 0, GRID_H - NA_WIN_R)
        start = r0 - r + (NA_WIN_R - 1)
        koff = pl.multiple_of(r0 * GRID_W, GRID_W)
        q = q_ref[rl * GRID_W:(rl + 1) * GRID_W, :]
        zero = jnp.zeros_like(q)
        qs = jnp.concatenate([jnp.where(lane < NA_HEAD_DIM, q, zero),
                              jnp.where(lane >= NA_HEAD_DIM, q, zero)], axis=0)
        bias = jnp.concatenate([bias_ref[start, 0], bias_ref[start, 1]], axis=0)
        s = lax.dot_general(qs, k_ref[pl.ds(koff, n_loc), :], dn, preferred_element_type=F32) + bias
        sl_ref[rl % 2] = s
        mp = jnp.maximum(_lane_tile_reduce(jnp.maximum, s), _lane_tile_reduce(jnp.maximum, ctx_scores(rl)))
        return jnp.max(mp, axis=-1, keepdims=True), koff

    def value_stage(rl, m, koff):
        rows = slice(rl * GRID_W, (rl + 1) * GRID_W)
        e_ctx = jnp.exp2(ctx_scores(rl) - m).astype(BF16)
        pctx_ref[0, rows, :] = e_ctx[:GRID_W]
        pctx_ref[1, rows, :] = e_ctx[GRID_W:]
        e_loc = jnp.exp2(sl_ref[rl % 2] - m).astype(BF16)
        res = jnp.dot(e_loc, vaug_ref[pl.ds(koff, n_loc), :], preferred_element_type=F32)
        acc_ref[0, rows, :] = res[:GRID_W]
        acc_ref[1, rows, :] = res[GRID_W:]

    state = score_stage(0)
    for rl in range(rows_per_step):
        nxt = score_stage(rl + 1) if rl + 1 < rows_per_step else None
        value_stage(rl, *state)
        state = nxt
    outs = []
    for hh in range(2):
        tot = acc_ref[hh] + jnp.dot(pctx_ref[hh], cvaug, preferred_element_type=F32)
        outs.append(tot[:, :LANES] * (1.0 / tot[:, LANES:]))
    o = jnp.where(lane_all < NA_HEAD_DIM, outs[0], outs[1])
    o_ref[...] = (o * _silu(gate_ref[...].astype(F32))).astype(BF16)


def _na_bias_tables(rpb):
    c = np.arange(GRID_W)
    c0 = np.clip(c - NA_WIN_C // 2, 0, GRID_W - NA_WIN_C)
    in_win = (c[None, :] >= c0[:, None]) & (c[None, :] < c0[:, None] + NA_WIN_C)
    period = 2 * GRID_W - 1
    lead = rpb.shape[:-1]
    w = jnp.concatenate([rpb[..., NA_WIN_C - 1:], jnp.zeros(lead + (period - (2 * NA_WIN_C - 1),), rpb.dtype),
                         rpb[..., :NA_WIN_C - 1]], axis=-1)
    flat = jnp.tile(w, (1,) * len(lead) + (GRID_W,))[..., :GRID_W * (period - 1)]
    toe = flat.reshape(lead + (GRID_W, period - 1))[..., :GRID_W]
    toe = jnp.where(in_win, toe.astype(F32) * LOG2E, -1e30)
    tabs = jnp.stack([toe[:, :, s:s + NA_WIN_R] for s in range(NA_WIN_R)], axis=1)
    tabs = jnp.transpose(tabs, (0, 1, 2, 4, 3, 5))
    return tabs.reshape(DEPTH, NA_WIN_R, NA_HEADS, GRID_W, NA_WIN_R * GRID_W)


def _neighbourhood_attention(p3, ck4, cv4, bias, l, *, rows_per_step=8):
    b, s, _ = p3.shape
    cq, ck, cv, cg = (_OFF[n] // LANES for n in ("nq", "nk", "nv", "n_gate"))
    n_ctx = ck4.shape[2]
    tq = rows_per_step * GRID_W
    n_loc = NA_WIN_R * GRID_W
    return pl.pallas_call(
        functools.partial(_na_attn_kernel, rows_per_step=rows_per_step),
        grid=(b, NA_HEADS // 2, GRID_H // rows_per_step),
        in_specs=[pl.BlockSpec((None, tq, LANES), lambda bi, hp, rb: (bi, rb, cq + hp)),
                  pl.BlockSpec((None, s, LANES), lambda bi, hp, rb: (bi, 0, ck + hp)),
                  pl.BlockSpec((None, s, LANES), lambda bi, hp, rb: (bi, 0, cv + hp)),
                  pl.BlockSpec((None, tq, LANES), lambda bi, hp, rb: (bi, rb, cg + hp)),
                  pl.BlockSpec((None, None, n_ctx, LANES), lambda bi, hp, rb: (bi, l, 0, hp)),
                  pl.BlockSpec((None, None, n_ctx, LANES), lambda bi, hp, rb: (bi, l, 0, hp)),
                  pl.BlockSpec((None, NA_WIN_R, 2, GRID_W, n_loc), lambda bi, hp, rb: (l, 0, hp, 0, 0))],
        out_specs=pl.BlockSpec((None, tq, LANES), lambda bi, hp, rb: (bi, rb, hp)),
        out_shape=jax.ShapeDtypeStruct((b, s, NA_HEADS * NA_HEAD_DIM), BF16),
        scratch_shapes=[pltpu.VMEM((2, tq, n_ctx), F32), pltpu.VMEM((2, tq, n_ctx), BF16),
                        pltpu.VMEM((2, 2 * GRID_W, n_loc), F32), pltpu.VMEM((s, 2 * LANES), BF16),
                        pltpu.VMEM((2, tq, 2 * LANES), F32)],
        compiler_params=_params(("arbitrary", "arbitrary", "arbitrary")),
        name="na_attn_latent",
    )(p3, p3, p3, p3, ck4, cv4, bias)


def _dft_tables():
    gd = FNET_GROUP_DIM
    n = np.arange(gd)
    ang = 2.0 * np.pi * ((n[:, None] * n[None, :]) % gd) / gd
    fc = np.concatenate([np.cos(ang), -np.sin(ang)], axis=1) / 16.0
    fl = np.concatenate([np.cos(ang), np.sin(ang)], axis=1) / 16.0
    npos = GRID_H * GRID_W
    k1 = np.arange(64)[None, :, None]
    col = np.arange(64)[:, None, None]
    row = np.arange(64)[None, None, :]
    a1 = 2.0 * np.pi * ((k1 * (64 * row + col)) % npos) / npos
    t1 = np.concatenate([np.cos(a1), np.sin(a1)], axis=2) / 8.0
    k = np.arange(64)
    a2 = 2.0 * np.pi * ((k[:, None] * k[None, :]) % 64) / 64.0
    t2 = np.concatenate([np.cos(a2), np.sin(a2)], axis=1) / 8.0
    cast = lambda a: jnp.asarray(a.astype(np.float32)).astype(BF16)
    return cast(fc), cast(fl), cast(t1), cast(t2)


def _fft_prompt_kernel(h_ref, fg_ref, fc_ref, fl_ref, o_ref):
    gd = FNET_GROUP_DIM
    for g in range(FNET_GROUPS):
        cols = slice(g * gd, (g + 1) * gd)
        z = jnp.dot(h_ref[:, cols], fc_ref[...], preferred_element_type=F32)
        stack = jnp.concatenate([z[:, :gd], z[:, gd:]], axis=0).astype(BF16)
        y = jnp.dot(fl_ref[...], stack, preferred_element_type=F32)
        o_ref[:, cols] = (y * _silu(fg_ref[:, cols].astype(F32))).astype(BF16)


def _fourier_prompt(h2d, p2d, fc, fl, b, s):
    gd = FNET_GROUP_DIM
    cf = _OFF["f_gate"] // D_MODEL
    return pl.pallas_call(
        _fft_prompt_kernel,
        grid=(b,),
        in_specs=[pl.BlockSpec((s, D_MODEL), lambda bi: (bi, 0)),
                  pl.BlockSpec((s, D_MODEL), lambda bi: (bi, cf)),
                  pl.BlockSpec((gd, 2 * gd), lambda bi: (0, 0)),
                  pl.BlockSpec((s, 2 * s), lambda bi: (0, 0))],
        out_specs=pl.BlockSpec((s, D_MODEL), lambda bi: (bi, 0)),
        out_shape=jax.ShapeDtypeStruct((b * s, D_MODEL), BF16),
        compiler_params=_params(("arbitrary",)),
        name="fourier_prompt",
    )(h2d, p2d, fc, fl)


FFT_PITCH = GRID_W + 4


def _fft_latent_kernel(h_ref, fg_ref, fc_ref, t1_ref, t2_ref, o_ref, z_ref, a_ref, y_ref):
    gd = FNET_GROUP_DIM
    rows = 512
    nt = 2 * gd // LANES
    pitch = FFT_PITCH
    for c in range(GRID_H * GRID_W // rows):
        z = jnp.dot(h_ref[c * rows:(c + 1) * rows, :], fc_ref[...], preferred_element_type=F32)
        for rl in range(rows // GRID_W):
            dst = (c * (rows // GRID_W) + rl) * pitch
            for t in range(nt):
                z_ref[t, dst:dst + GRID_W, :] = z[rl * GRID_W:(rl + 1) * GRID_W, t * LANES:(t + 1) * LANES]

    for col in range(GRID_W):
        tiles = [z_ref[t, pl.ds(col, GRID_H, stride=pitch), :] for t in range(nt)]
        zr = jnp.concatenate(tiles[:nt // 2], axis=1)
        zi = jnp.concatenate(tiles[nt // 2:], axis=1)
        stack = jnp.concatenate([jnp.concatenate([zr, zi], axis=1),
                                 jnp.concatenate([zi, -zr], axis=1)], axis=0).astype(BF16)
        a = jnp.dot(t1_ref[col], stack, preferred_element_type=F32)
        for t in range(nt):
            a_ref[t, col * pitch:col * pitch + GRID_H, :] = a[:, t * LANES:(t + 1) * LANES]

    for k1 in range(GRID_H):
        tiles = [a_ref[t, pl.ds(k1, GRID_W, stride=pitch), :] for t in range(nt)]
        stack = jnp.concatenate([jnp.concatenate(tiles[:nt // 2], axis=1),
                                 jnp.concatenate(tiles[nt // 2:], axis=1)], axis=0).astype(BF16)
        y = jnp.dot(t2_ref[...], stack, preferred_element_type=F32)
        for t in range(nt // 2):
            y_ref[t, pl.ds(k1, GRID_H, stride=pitch), :] = y[:, t * LANES:(t + 1) * LANES]

    for k2 in range(GRID_H):
        y = jnp.concatenate([y_ref[t, k2 * pitch:k2 * pitch + GRID_W, :] for t in range(nt // 2)], axis=1)
        sl = slice(k2 * GRID_W, (k2 + 1) * GRID_W)
        o_ref[sl, :] = (y * _silu(fg_ref[sl, :].astype(F32))).astype(BF16)


def _fourier_latent(h2d, p2d, fc, t1, t2, b):
    gd = FNET_GROUP_DIM
    n = GRID_H * GRID_W
    npad = GRID_H * FFT_PITCH
    cf = _OFF["f_gate"] // gd
    return pl.pallas_call(
        _fft_latent_kernel,
        grid=(b, FNET_GROUPS),
        in_specs=[pl.BlockSpec((n, gd), lambda bi, g: (bi, g)),
                  pl.BlockSpec((n, gd), lambda bi, g: (bi, cf + g)),
                  pl.BlockSpec((gd, 2 * gd), lambda bi, g: (0, 0)),
                  pl.BlockSpec((GRID_W, GRID_H, 2 * GRID_H), lambda bi, g: (0, 0, 0)),
                  pl.BlockSpec((GRID_H, 2 * GRID_W), lambda bi, g: (0, 0))],
        out_specs=pl.BlockSpec((n, gd), lambda bi, g: (bi, g)),
        out_shape=jax.ShapeDtypeStruct((b * n, D_MODEL), BF16),
        scratch_shapes=[pltpu.VMEM((2 * gd // LANES, npad, LANES), F32), pltpu.VMEM((2 * gd // LANES, npad, LANES), F32),
                        pltpu.VMEM((gd // LANES, npad, LANES), F32)],
        compiler_params=_params(("arbitrary", "arbitrary")),
        name="fourier_latent",
    )(h2d, p2d, fc, t1, t2)


def _merge_kernel(x_ref, mod_ref, mg_ref, u_ref, v_ref, sgate_ref, yd_ref, yn_ref, yf_ref,
                  gv_ref, sw_ref, sb_ref, psg_ref, pd_ref, pn_ref, pf_ref, wo_ref, gpost_ref,
                  o_ref, ysg_ref, *, latent, tm, tiles_per_batch):
    i = pl.program_id(0)
    lane = lax.broadcasted_iota(jnp.int32, (CHUNK, LANES), 1)
    for c in range(tm // CHUNK):
        sl = slice(c * CHUNK, (c + 1) * CHUNK)
        v = v_ref[sl, :].astype(F32)
        vn = (v * lax.rsqrt(jnp.mean(v * v, axis=-1, keepdims=True) + EPS) * gv_ref[...]).astype(BF16)
        parts = []
        for pr in range(SG_GROUPS // 2):
            vp = vn[:, pr * LANES:(pr + 1) * LANES]
            s0 = jnp.dot(sw_ref[2 * pr], vp, preferred_element_type=F32)
            s1 = jnp.dot(sw_ref[2 * pr + 1], vp, preferred_element_type=F32)
            parts.append(jnp.where(lane < SG_WIDTH // SG_GROUPS, s0, s1))
        s = jnp.concatenate(parts, axis=1) + sb_ref[...]
        y = u_ref[sl, :].astype(F32) * s * _silu(sgate_ref[sl, :].astype(F32))
        ysg_ref[sl, :] = y.astype(BF16)

    def gate(n):
        return jax.nn.sigmoid(mg_ref[:, n * D_MODEL:(n + 1) * D_MODEL].astype(F32))

    mixed = gate(0) * jnp.dot(ysg_ref[...], psg_ref[...], preferred_element_type=F32)
    mixed = mixed + gate(1) * jnp.dot(yd_ref[...], pd_ref[...], preferred_element_type=F32)
    mixed = mixed + gate(2) * jnp.dot(yn_ref[...], pn_ref[...], preferred_element_type=F32)
    mixed = mixed + gate(3) * jnp.dot(yf_ref[...], pf_ref[...], preferred_element_type=F32)
    out = jnp.dot(mixed.astype(BF16), wo_ref[...], preferred_element_type=F32)
    out = out * lax.rsqrt(jnp.mean(out * out, axis=-1, keepdims=True) + EPS) * gpost_ref[...]
    row = (1 + i // tiles_per_batch) if latent else 0
    g = mod_ref[pl.ds(row, 1), :][:, 2 * D_MODEL:]
    o_ref[...] = x_ref[...] + g * out


def _merge(x2d, mod_all, p2d, yd, yn, yf, w, l, *, latent, seq, tm=512):
    t = x2d.shape[0]
    cu, cv, cs = (_OFF[n] // SG_WIDTH for n in ("sg_u", "sg_v", "sg_gate"))
    tok = lambda width, c=0: pl.BlockSpec((tm, width), lambda i: (i, c))
    wspec = lambda *shape: pl.BlockSpec((None,) + shape, lambda i: (l,) + (0,) * len(shape))
    tiles_per_batch = seq // tm if latent else 1
    return pl.pallas_call(
        functools.partial(_merge_kernel, latent=latent, tm=tm, tiles_per_batch=tiles_per_batch),
        grid=(t // tm,),
        in_specs=[tok(D_MODEL), wspec(8, 3 * D_MODEL), tok(N_BRANCH * D_MODEL, 0),
                  tok(SG_WIDTH, cu), tok(SG_WIDTH, cv), tok(SG_WIDTH, cs),
                  tok(512), tok(512), tok(D_MODEL),
                  wspec(1, SG_WIDTH), wspec(SG_GROUPS, CHUNK, CHUNK), wspec(CHUNK, SG_WIDTH),
                  wspec(SG_WIDTH, D_MODEL), wspec(512, D_MODEL), wspec(512, D_MODEL),
                  wspec(D_MODEL, D_MODEL), wspec(D_MODEL, D_MODEL), wspec(1, D_MODEL)],
        out_specs=tok(D_MODEL),
        out_shape=jax.ShapeDtypeStruct((t, D_MODEL), F32),
        scratch_shapes=[pltpu.VMEM((tm, SG_WIDTH), BF16)],
        compiler_params=_params(("arbitrary",)),
        name="merge_latent" if latent else "merge_prompt",
    )(x2d, mod_all, p2d, p2d, p2d, p2d, yd, yn, yf,
      w["sg_norm_g"], w["sg_w"], w["sg_b"], w["p_sg"], w["p_diff"], w["p_na"], w["p_fnet"], w["w_out"],
      w["g_post"])


def _rope_tables(n_tokens):
    nf = DIFF_HEAD_DIM // 4
    t = np.arange(n_tokens)
    inv = ROPE_BASE ** (-np.arange(nf, dtype=np.float64) / nf)
    ang_r = (t // GRID_W).astype(np.float64)[:, None] * inv[None, :]
    ang_c = (t % GRID_W).astype(np.float64)[:, None] * inv[None, :]
    z = np.zeros_like(ang_r)
    cos64 = np.concatenate([np.cos(ang_r), np.cos(ang_r), np.cos(ang_c), np.cos(ang_c)], axis=1)
    sina64 = np.concatenate([-np.sin(ang_r), z, -np.sin(ang_c), z], axis=1)
    sinb64 = np.concatenate([z, np.sin(ang_r), z, np.sin(ang_c)], axis=1)
    two = lambda a: jnp.asarray(np.concatenate([a, a], axis=1).astype(np.float32))
    return two(cos64), two(sina64), two(sinb64)


def kernel(x_prompt, x_sample, cache_diff_k, cache_diff_v, cache_na_k, cache_na_v, c, c_ctx,
           w_mod, b_mod, g_pre, g_post, w_in, sg_norm_g, sg_w, sg_b,
           diff_lam_q1, diff_lam_k1, diff_lam_q2, diff_lam_k2, diff_subln_g, na_rpb,
           w_proj_sg, w_proj_diff, w_proj_na, w_proj_fnet, w_out):
    bp, sp, _ = x_prompt.shape
    bs, ss, _ = x_sample.shape
    n_ctx = cache_diff_k.shape[2]

    cond8 = jnp.zeros((8, D_MODEL), F32).at[0].set(c_ctx).at[1:1 + bs].set(c)
    mod_all = _modulation(cond8, w_mod, b_mod)

    w_in_b = w_in.astype(BF16)
    weights = {
        "sg_norm_g": sg_norm_g.reshape(DEPTH, 1, SG_WIDTH),
        "sg_w": sg_w.astype(BF16),
        "sg_b": jnp.repeat(jnp.transpose(sg_b, (0, 2, 1)), SG_WIDTH // SG_GROUPS, axis=2),
        "p_sg": w_proj_sg.astype(BF16), "p_diff": w_proj_diff.astype(BF16),
        "p_na": w_proj_na.astype(BF16), "p_fnet": w_proj_fnet.astype(BF16),
        "w_out": w_out.astype(BF16), "g_post": g_post.reshape(DEPTH, 1, D_MODEL),
    }
    g_pre3 = g_pre.reshape(DEPTH, 1, D_MODEL)
    lamp = jnp.stack([diff_lam_q1, diff_lam_k1, diff_lam_q2, diff_lam_k2], axis=1)
    subg3 = diff_subln_g.reshape(DEPTH, 1, 2 * DIFF_HEAD_DIM)
    rope_tabs = _rope_tables(ss)
    fc, fl, t1, t2 = _dft_tables()
    bias = _na_bias_tables(na_rpb)
    ckd = cache_diff_k.reshape(bs, DEPTH, n_ctx, DIFF_HEADS * 2 * DIFF_HEAD_DIM)
    cvd = cache_diff_v.reshape(bs, DEPTH, n_ctx, DIFF_HEADS * 2 * DIFF_HEAD_DIM)
    ckn = cache_na_k.reshape(bs, DEPTH, n_ctx, NA_HEADS * NA_HEAD_DIM)
    cvn = cache_na_v.reshape(bs, DEPTH, n_ctx, NA_HEADS * NA_HEAD_DIM)

    xp = x_prompt.reshape(bp * sp, D_MODEL)
    xs = x_sample.reshape(bs * ss, D_MODEL)
    caches = None
    for l in range(DEPTH):
        pp, hp, *caches = _inproj(xp, mod_all, g_pre3, w_in_b, l, latent=False, tm=4 * sp, seq=sp, n_col_tiles=7,
                                  cache_bufs=caches)
        pp3 = pp.reshape(bp, sp, IN_WIDTH)
        yd = _diff_attention(pp3, lamp, subg3, l, tq=sp, heads=DIFF_HEADS).reshape(bp * sp, -1)
        yn = _context_attention(pp3).reshape(bp * sp, -1)
        yf = _fourier_prompt(hp, pp, fc, fl, bp, sp)
        xp = _merge(xp, mod_all, pp, yd, yn, yf, weights, l, latent=False, seq=sp)
        ps, hs = _inproj(xs, mod_all, g_pre3, w_in_b, l, latent=True, tm=1024, seq=ss, n_col_tiles=3,
                         rope_tabs=rope_tabs)
        ps3 = ps.reshape(bs, ss, IN_WIDTH)
        yd = _diff_attention(ps3, lamp, subg3, l, tq=2048, heads=1, sub=8, ctx=(ckd, cvd)).reshape(bs * ss, -1)
        yn = _neighbourhood_attention(ps3, ckn, cvn, bias, l).reshape(bs * ss, -1)
        yf = _fourier_latent(hs, ps, fc, t1, t2, bs)
        xs = _merge(xs, mod_all, ps, yd, yn, yf, weights, l, latent=True, seq=ss)

    dk5, dv5, nk5, nv5 = caches
    return (xp.reshape(bp, sp, D_MODEL), xs.reshape(bs, ss, D_MODEL),
            dk5, dv5,
            nk5.reshape(bp, DEPTH, sp, NA_HEADS, NA_HEAD_DIM),
            nv5.reshape(bp, DEPTH, sp, NA_HEADS, NA_HEAD_DIM))
```

```python
import functools
import math

import numpy as np
import jax
import jax.numpy as jnp
from jax import lax
from jax.experimental import pallas as pl
from jax.experimental.pallas import tpu as pltpu

F32 = jnp.float32
BF16 = jnp.bfloat16

D_MODEL = 1024
DEPTH = 4
GRID_W = 64
GRID_H = 64
EPS = 1e-6
ROPE_BASE = 10000.0
CHUNK = 128
SG_WIDTH = 512
SG_GROUPS = 8
DIFF_HEADS = 4
DIFF_HEAD_DIM = 64
NA_HEADS = 8
NA_HEAD_DIM = 64
NA_WIN_R = 8
NA_WIN_C = 16
FNET_GROUPS = 4
FNET_GROUP_DIM = 256
N_BRANCH = 4
LANES = 128
VMEM_LIMIT = 56 * 1024 * 1024
LOG2E = math.log2(math.e)

_REF_SEGMENTS = (("sg_u", 512), ("sg_v", 512), ("sg_gate", 512),
                 ("dq", 512), ("dk", 512), ("dv", 512), ("d_gate", 512),
                 ("nq", 512), ("nk", 512), ("nv", 512), ("n_gate", 512),
                 ("f_gate", 1024), ("merge", 4096))
_MY_ORDER = ("merge", "f_gate", "sg_u", "sg_v", "sg_gate", "dq", "dk", "dv", "d_gate",
             "nq", "nk", "nv", "n_gate")
UNIT = 512


def _layout():
    ref_off, o = {}, 0
    for name, width in _REF_SEGMENTS:
        ref_off[name] = (o, width)
        o += width
    my_off, o, unit_names = {}, 0, []
    for name in _MY_ORDER:
        my_off[name] = o
        o += ref_off[name][1]
        unit_names += [name] * (ref_off[name][1] // UNIT)
    return ref_off, my_off, o, tuple(unit_names)


_REF_OFF, _OFF, IN_WIDTH, _UNIT_NAMES = _layout()


def _ref_unit(n):
    merge_units = _REF_OFF["merge"][1] // UNIT
    gate_units = _REF_OFF["f_gate"][1] // UNIT
    return jnp.where(n < merge_units, n + _REF_OFF["merge"][0] // UNIT,
                     jnp.where(n < merge_units + gate_units, n - merge_units + _REF_OFF["f_gate"][0] // UNIT,
                               n - merge_units - gate_units))


def _silu(x):
    return x * jax.nn.sigmoid(x)


def _params(sem):
    return pltpu.CompilerParams(dimension_semantics=sem, vmem_limit_bytes=VMEM_LIMIT)


def _mod_kernel(cond_ref, w_ref, b_ref, o_ref):
    s = _silu(cond_ref[...])
    o_ref[...] = jnp.dot(s, w_ref[...], precision=lax.Precision.HIGHEST,
                         preferred_element_type=F32) + b_ref[...]


def _modulation(cond8, w_mod, b_mod):
    b3 = b_mod.reshape(DEPTH, 1, 3 * D_MODEL)
    return pl.pallas_call(
        _mod_kernel,
        grid=(DEPTH, 3),
        in_specs=[pl.BlockSpec((8, D_MODEL), lambda l, j: (0, 0)),
                  pl.BlockSpec((None, D_MODEL, D_MODEL), lambda l, j: (l, 0, j)),
                  pl.BlockSpec((None, 1, D_MODEL), lambda l, j: (l, 0, j))],
        out_specs=pl.BlockSpec((None, 8, D_MODEL), lambda l, j: (l, 0, j)),
        out_shape=jax.ShapeDtypeStruct((DEPTH, 8, 3 * D_MODEL), F32),
        compiler_params=_params(("arbitrary", "arbitrary")),
        name="modulation",
    )(cond8, w_mod, b3)


def _rope(acc, cos_ref, sina_ref, sinb_ref):
    cos, sina, sinb = cos_ref[...], sina_ref[...], sinb_ref[...]
    outs = []
    for hd in range(acc.shape[1] // LANES):
        a = acc[:, hd * LANES:(hd + 1) * LANES]
        up = pltpu.roll(a, LANES - 16, 1)
        dn = pltpu.roll(a, 16, 1)
        outs.append(a * cos + up * sina + dn * sinb)
    return jnp.concatenate(outs, axis=1)


def _inproj_kernel(*refs, latent, tm, seq, tiles_per_batch, n_alias, n_col_tiles):
    units_per_tile = len(_UNIT_NAMES) // n_col_tiles
    (x_ref, mod_ref, gpre_ref) = refs[:3]
    w_refs = refs[3:3 + units_per_tile]
    rest = refs[3 + units_per_tile:]
    if latent:
        (cos_ref, sina_ref, sinb_ref, p_ref, h_ref) = rest
        cache_refs = {}
    else:
        (p_ref, h_ref, dk_ref, dv_ref, nk_ref, nv_ref) = rest[n_alias:]
        cache_refs = {"dk": dk_ref, "dv": dv_ref, "nk": nk_ref, "nv": nv_ref}
    i = pl.program_id(0)
    j = pl.program_id(1)

    @pl.when(j == 0)
    def _():
        x = x_ref[...]
        r = lax.rsqrt(jnp.mean(x * x, axis=-1, keepdims=True) + EPS)
        row = (1 + i // tiles_per_batch) if latent else 0
        m = mod_ref[pl.ds(row, 1), :]
        shift = m[:, :D_MODEL]
        scale = m[:, D_MODEL:2 * D_MODEL]
        h_ref[...] = ((x * r * gpre_ref[...]) * (1.0 + scale) + shift).astype(BF16)

    def unit(u, name):
        cols = slice(u * UNIT, (u + 1) * UNIT)
        acc = jnp.dot(h_ref[...], w_refs[u][...], preferred_element_type=F32)
        if name in ("dk", "dv") and not latent:
            for hd in range(DIFF_HEADS):
                cache_refs[name][:, :, hd, :] = acc[:, hd * LANES:(hd + 1) * LANES].reshape(tm // seq, seq, LANES)
        elif name in cache_refs:
            cache_refs[name][...] = acc.reshape(tm // seq, seq, UNIT)
        if name in ("dq", "nq"):
            acc = acc * (DIFF_HEAD_DIM ** -0.5 * LOG2E)
        if latent and name in ("dq", "dk"):
            acc = _rope(acc, cos_ref, sina_ref, sinb_ref)
        p_ref[:, cols] = acc.astype(BF16)

    for jt in range(n_col_tiles):
        @pl.when(j == jt)
        def _(jt=jt):
            for u in range(units_per_tile):
                unit(u, _UNIT_NAMES[jt * units_per_tile + u])


def _inproj(x2d, mod_all, g_pre3, w_in_b, l, *, latent, tm, seq, n_col_tiles, rope_tabs=None, cache_bufs=None):
    t = x2d.shape[0]
    n_i = t // tm
    tn = IN_WIDTH // n_col_tiles
    units_per_tile = tn // UNIT
    in_specs = [pl.BlockSpec((tm, D_MODEL), lambda i, j: (i, 0)),
                pl.BlockSpec((None, 8, 3 * D_MODEL), lambda i, j: (l, 0, 0)),
                pl.BlockSpec((None, 1, D_MODEL), lambda i, j: (l, 0, 0))]
    args = [x2d, mod_all, g_pre3]
    for u in range(units_per_tile):
        in_specs.append(pl.BlockSpec((None, D_MODEL, UNIT),
                                     lambda i, j, u=u: (l, 0, _ref_unit(j * units_per_tile + u))))
        args.append(w_in_b)
    out_specs = [pl.BlockSpec((tm, tn), lambda i, j: (i, j)),
                 pl.BlockSpec((tm, D_MODEL), lambda i, j: (i, 0))]
    out_shape = [jax.ShapeDtypeStruct((t, IN_WIDTH), BF16),
                 jax.ShapeDtypeStruct((t, D_MODEL), BF16)]
    tiles_per_batch = 1
    aliases = {}
    n_alias = 0
    if latent:
        tiles_per_batch = seq // tm
        for tab in rope_tabs:
            in_specs.append(pl.BlockSpec((tm, LANES), lambda i, j: (i % tiles_per_batch, 0)))
            args.append(tab)
    else:
        nb = tm // seq
        if cache_bufs is not None:
            n_alias = len(cache_bufs)
            for k, buf in enumerate(cache_bufs):
                in_specs.append(pl.BlockSpec(memory_space=pl.ANY))
                args.append(buf)
                aliases[3 + units_per_tile + k] = 2 + k
        for _ in range(2):
            out_specs.append(pl.BlockSpec((nb, None, seq, DIFF_HEADS, LANES), lambda i, j: (i, l, 0, 0, 0)))
            out_shape.append(jax.ShapeDtypeStruct((t // seq, DEPTH, seq, DIFF_HEADS, LANES), F32))
        for _ in range(2):
            out_specs.append(pl.BlockSpec((nb, None, seq, UNIT), lambda i, j: (i, l, 0, 0)))
            out_shape.append(jax.ShapeDtypeStruct((t // seq, DEPTH, seq, UNIT), F32))
    return pl.pallas_call(
        functools.partial(_inproj_kernel, latent=latent, tm=tm, seq=seq, tiles_per_batch=tiles_per_batch,
                          n_alias=n_alias, n_col_tiles=n_col_tiles),
        grid=(n_i, n_col_tiles),
        in_specs=in_specs,
        out_specs=out_specs,
        out_shape=out_shape,
        input_output_aliases=aliases,
        compiler_params=_params(("arbitrary", "arbitrary")),
        name="inproj_latent" if latent else "inproj_prompt",
    )(*args)


def _lane_tile_reduce(op, x):
    parts = [x[:, t * LANES:(t + 1) * LANES] for t in range(x.shape[1] // LANES)]
    return functools.reduce(op, parts)


def _qk_chunk(q, k_ref, s_ref, kcols, c, kc):
    rows = slice(c * kc, (c + 1) * kc)
    s = lax.dot_general(q, k_ref[rows, kcols], (((1,), (1,)), ((), ())), preferred_element_type=F32)
    s_ref[:, rows] = s
    return _lane_tile_reduce(jnp.maximum, s)


def _pv_chunk(s_ref, m, vaug, c, kc):
    rows = slice(c * kc, (c + 1) * kc)
    e = jnp.exp2(s_ref[:, rows] - m).astype(BF16)
    return jnp.dot(e, vaug[rows], preferred_element_type=F32)


def _pipelined_attention(streams, k_ref, s_refs, n_keys, kc):
    nck = n_keys // kc
    acc = lambda a, b: b if a is None else a + b
    mx = lambda a, b: b if a is None else jnp.maximum(a, b)
    outs = []
    if nck == 1:
        for q, cols, vaug in streams:
            s = lax.dot_general(q, k_ref[:, cols], (((1,), (1,)), ((), ())), preferred_element_type=F32)
            e = jnp.exp2(s - jnp.max(s, axis=-1, keepdims=True)).astype(BF16)
            r = jnp.dot(e, vaug, preferred_element_type=F32)
            outs.append(r[:, :LANES] * (1.0 / r[:, LANES:]))
        return outs
    mp = None
    for c in range(nck):
        mp = mx(mp, _qk_chunk(streams[0][0], k_ref, s_refs[0], streams[0][1], c, kc))
    for i, (_, _, vaug) in enumerate(streams):
        m = jnp.max(mp, axis=-1, keepdims=True)
        r = mp = None
        for c in range(nck):
            r = acc(r, _pv_chunk(s_refs[i % 2], m, vaug, c, kc))
            if i + 1 < len(streams):
                q_n, cols_n, _ = streams[i + 1]
                mp = mx(mp, _qk_chunk(q_n, k_ref, s_refs[(i + 1) % 2], cols_n, c, kc))
        outs.append(r[:, :LANES] * (1.0 / r[:, LANES:]))
    return outs


def _diff_attn_kernel(*refs, lam_init, n_self, has_ctx, heads, kc, sub):
    if has_ctx:
        (lamp_ref, subg_ref, q_ref, k_ref, v_ref, gate_ref, ck_ref, cv_ref, o_ref,
         s1_ref, s2_ref, kc_ref, vc_ref) = refs

        @pl.when(pl.program_id(2) == 0)
        def _():
            kc_ref[0:n_self, :] = k_ref[...]
            kc_ref[n_self:, :] = ck_ref[...].astype(BF16)
            vc_ref[0:n_self, :LANES] = v_ref[...]
            vc_ref[n_self:, :LANES] = cv_ref[...].astype(BF16)
            vc_ref[:, LANES:] = jnp.ones((vc_ref.shape[0], LANES), BF16)

        k_ref = kc_ref
    else:
        (lamp_ref, subg_ref, q_ref, k_ref, v_ref, gate_ref, o_ref, s1_ref, s2_ref) = refs
    n_keys = k_ref.shape[0]
    tq = q_ref.shape[0] // sub
    lp = lamp_ref[...]
    lam = (jnp.exp(jnp.sum(lp[0:1] * lp[1:2], axis=-1, keepdims=True))
           - jnp.exp(jnp.sum(lp[2:3] * lp[3:4], axis=-1, keepdims=True)) + lam_init)
    lane = lax.broadcasted_iota(jnp.int32, (tq, LANES), 1)
    zero = jnp.zeros((tq, LANES), BF16)
    streams, where = [], []
    for hd in range(heads):
        cols = slice(hd * LANES, (hd + 1) * LANES)
        if has_ctx:
            vaug = vc_ref[...]
        else:
            vaug = jnp.concatenate([v_ref[:, cols], jnp.ones((n_keys, LANES), BF16)], axis=1)
        for sb in range(sub):
            rows = slice(sb * tq, (sb + 1) * tq)
            q = q_ref[rows, cols]
            streams.append((jnp.where(lane < DIFF_HEAD_DIM, q, zero), cols, vaug))
            streams.append((jnp.where(lane >= DIFF_HEAD_DIM, q, zero), cols, vaug))
            where.append((rows, cols))
    outs = _pipelined_attention(streams, k_ref, (s1_ref, s2_ref), n_keys, kc)
    for n, (rows, cols) in enumerate(where):
        o = outs[2 * n] - lam * outs[2 * n + 1]
        y = o * lax.rsqrt(jnp.mean(o * o, axis=-1, keepdims=True) + EPS) * subg_ref[...]
        y = y * (1.0 - lam_init)
        o_ref[rows, cols] = (y * _silu(gate_ref[rows, cols].astype(F32))).astype(BF16)


def _diff_attention(p3, lamp, subg3, l, *, tq, heads, ctx=None, sub=1):
    b, s, _ = p3.shape
    w = heads * LANES
    cq, ck, cv, cg = (_OFF[n] // w for n in ("dq", "dk", "dv", "d_gate"))
    lam_init = 0.8 - 0.6 * math.exp(-0.3 * l)
    in_specs = [pl.BlockSpec((None, 4, DIFF_HEAD_DIM), lambda bi, h, qi: (l, 0, 0)),
                pl.BlockSpec((None, 1, LANES), lambda bi, h, qi: (l, 0, 0)),
                pl.BlockSpec((None, tq, w), lambda bi, h, qi: (bi, qi, cq + h)),
                pl.BlockSpec((None, s, w), lambda bi, h, qi: (bi, 0, ck + h)),
                pl.BlockSpec((None, s, w), lambda bi, h, qi: (bi, 0, cv + h)),
                pl.BlockSpec((None, tq, w), lambda bi, h, qi: (bi, qi, cg + h))]
    args = [lamp, subg3, p3, p3, p3, p3]
    n_keys = s
    scratch = []
    if ctx is not None:
        ck4, cv4 = ctx
        n_ctx = ck4.shape[2]
        n_keys = s + n_ctx
        for arr in (ck4, cv4):
            in_specs.append(pl.BlockSpec((None, None, n_ctx, w), lambda bi, h, qi: (bi, l, 0, h)))
            args.append(arr)
        scratch = [pltpu.VMEM((n_keys, w), BF16), pltpu.VMEM((n_keys, 2 * w), BF16)]
    kc = min(n_keys, 512)
    scratch = [pltpu.VMEM((tq // sub, n_keys), F32), pltpu.VMEM((tq // sub, n_keys), F32)] + scratch
    return pl.pallas_call(
        functools.partial(_diff_attn_kernel, lam_init=lam_init, n_self=s, has_ctx=ctx is not None,
                          heads=heads, kc=kc, sub=sub),
        grid=(b, DIFF_HEADS // heads, s // tq),
        in_specs=in_specs,
        out_specs=pl.BlockSpec((None, tq, w), lambda bi, h, qi: (bi, qi, h)),
        out_shape=jax.ShapeDtypeStruct((b, s, DIFF_HEADS * LANES), BF16),
        scratch_shapes=scratch,
        compiler_params=_params(("arbitrary", "arbitrary", "arbitrary")),
        name="diff_attn_latent" if ctx is not None else "diff_attn_prompt",
    )(*args)


def _ctx_attn_kernel(q_ref, k_ref, v_ref, gate_ref, o_ref):
    dn = (((1,), (1,)), ((), ()))
    n = k_ref.shape[0]
    ones = jnp.ones((n, LANES), BF16)
    for hp in range(NA_HEADS // 2):
        cols = slice(hp * LANES, (hp + 1) * LANES)
        q = q_ref[:, cols]
        lane = lax.broadcasted_iota(jnp.int32, q.shape, 1)
        zero = jnp.zeros_like(q)
        qs = jnp.concatenate([jnp.where(lane < NA_HEAD_DIM, q, zero), jnp.where(lane >= NA_HEAD_DIM, q, zero)], axis=0)
        s = lax.dot_general(qs, k_ref[:, cols], dn, preferred_element_type=F32)
        e = jnp.exp2(s - jnp.max(s, axis=-1, keepdims=True)).astype(BF16)
        r = jnp.dot(e, jnp.concatenate([v_ref[:, cols], ones], axis=1), preferred_element_type=F32)
        o2 = r[:, :LANES] * (1.0 / r[:, LANES:])
        o = jnp.where(lane < NA_HEAD_DIM, o2[:n], o2[n:])
        o_ref[:, cols] = (o * _silu(gate_ref[:, cols].astype(F32))).astype(BF16)


def _prompt_mixers_kernel(lamp_ref, subg_ref, dq_ref, dk_ref, dv_ref, dg_ref, nq_ref, nk_ref, nv_ref, ng_ref,
                          h_ref, fg_ref, fc_ref, fl_ref, yd_ref, yn_ref, yf_ref, *, lam_init):
    _diff_attn_kernel(lamp_ref, subg_ref, dq_ref, dk_ref, dv_ref, dg_ref, yd_ref, None, None,
                      lam_init=lam_init, n_self=dk_ref.shape[0], has_ctx=False, heads=DIFF_HEADS,
                      kc=dk_ref.shape[0], sub=1)
    _ctx_attn_kernel(nq_ref, nk_ref, nv_ref, ng_ref, yn_ref)
    _fft_prompt_kernel(h_ref, fg_ref, fc_ref, fl_ref, yf_ref)


def _prompt_mixers(p3, h2d, lamp, subg3, fc, fl, l):
    b, s, _ = p3.shape
    w = DIFF_HEADS * LANES
    seg = lambda name: pl.BlockSpec((None, s, w), lambda bi: (bi, 0, _OFF[name] // w))
    tok = lambda width: pl.BlockSpec((None, s, width), lambda bi: (bi, 0, 0))
    lam_init = 0.8 - 0.6 * math.exp(-0.3 * l)
    return pl.pallas_call(
        functools.partial(_prompt_mixers_kernel, lam_init=lam_init),
        grid=(b,),
        in_specs=[pl.BlockSpec((None, 4, DIFF_HEAD_DIM), lambda bi: (l, 0, 0)),
                  pl.BlockSpec((None, 1, LANES), lambda bi: (l, 0, 0))]
                 + [seg(n) for n in ("dq", "dk", "dv", "d_gate", "nq", "nk", "nv", "n_gate")]
                 + [tok(D_MODEL),
                    pl.BlockSpec((None, s, D_MODEL), lambda bi: (bi, 0, _OFF["f_gate"] // D_MODEL)),
                    pl.BlockSpec((FNET_GROUP_DIM, 2 * FNET_GROUP_DIM), lambda bi: (0, 0)),
                    pl.BlockSpec((s, 2 * s), lambda bi: (0, 0))],
        out_specs=[tok(w), tok(w), tok(D_MODEL)],
        out_shape=[jax.ShapeDtypeStruct((b, s, w), BF16), jax.ShapeDtypeStruct((b, s, w), BF16),
                   jax.ShapeDtypeStruct((b, s, D_MODEL), BF16)],
        compiler_params=_params(("arbitrary",)),
        name="prompt_mixers",
    )(lamp, subg3, *([p3] * 8), h2d.reshape(b, s, D_MODEL), p3, fc, fl)


def _na_attn_kernel(q_ref, k_ref, v_ref, gate_ref, ck_ref, cv_ref, bias_ref, o_ref,
                    sctx_ref, pctx_ref, sl_ref, vaug_ref, acc_ref, *, rows_per_step):
    rb = pl.program_id(2)
    n_loc = NA_WIN_R * GRID_W
    ones = jnp.ones((n_loc, LANES), BF16)

    @pl.when(rb == 0)
    def _():
        vaug_ref[:, :LANES] = v_ref[...]
        vaug_ref[:, LANES:] = jnp.ones((vaug_ref.shape[0], LANES), BF16)

    ckb = ck_ref[...].astype(BF16)
    cvaug = jnp.concatenate([cv_ref[...].astype(BF16), ones], axis=1)
    dn = (((1,), (1,)), ((), ()))
    q_all = q_ref[...]
    lane_all = lax.broadcasted_iota(jnp.int32, q_all.shape, 1)
    zero_all = jnp.zeros_like(q_all)
    for hh in range(2):
        sel = (lane_all < NA_HEAD_DIM) if hh == 0 else (lane_all >= NA_HEAD_DIM)
        sctx_ref[hh] = lax.dot_general(jnp.where(sel, q_all, zero_all), ckb, dn, preferred_element_type=F32)
    lane = lax.broadcasted_iota(jnp.int32, (GRID_W, LANES), 1)

    def ctx_scores(rl):
        rows = slice(rl * GRID_W, (rl + 1) * GRID_W)
        return jnp.concatenate([sctx_ref[0, rows, :], sctx_ref[1, rows, :]], axis=0)

    def score_stage(rl):
        r = rb * rows_per_step + rl
        r0 = jnp.clip(r - NA_WIN_R // 2, 0, GRID_H - NA_WIN_R)
        start = r0 - r + (NA_WIN_R - 1)
        koff = pl.multiple_of(r0 * GRID_W, GRID_W)
        q = q_ref[rl * GRID_W:(rl + 1) * GRID_W, :]
        zero = jnp.zeros_like(q)
        qs = jnp.concatenate([jnp.where(lane < NA_HEAD_DIM, q, zero),
                              jnp.where(lane >= NA_HEAD_DIM, q, zero)], axis=0)
        bias = jnp.concatenate([bias_ref[start, 0], bias_ref[start, 1]], axis=0)
        s = lax.dot_general(qs, k_ref[pl.ds(koff, n_loc), :], dn, preferred_element_type=F32) + bias
        sl_ref[rl % 2] = s
        mp = jnp.maximum(_lane_tile_reduce(jnp.maximum, s), _lane_tile_reduce(jnp.maximum, ctx_scores(rl)))
        return jnp.max(mp, axis=-1, keepdims=True), koff

    def value_stage(rl, m, koff):
        rows = slice(rl * GRID_W, (rl + 1) * GRID_W)
        e_ctx = jnp.exp2(ctx_scores(rl) - m).astype(BF16)
        pctx_ref[0, rows, :] = e_ctx[:GRID_W]
        pctx_ref[1, rows, :] = e_ctx[GRID_W:]
        e_loc = jnp.exp2(sl_ref[rl % 2] - m).astype(BF16)
        res = jnp.dot(e_loc, vaug_ref[pl.ds(koff, n_loc), :], preferred_element_type=F32)
        acc_ref[0, rows, :] = res[:GRID_W]
        acc_ref[1, rows, :] = res[GRID_W:]

    state = score_stage(0)
    for rl in range(rows_per_step):
        nxt = score_stage(rl + 1) if rl + 1 < rows_per_step else None
        value_stage(rl, *state)
        state = nxt
    outs = []
    for hh in range(2):
        tot = acc_ref[hh] + jnp.dot(pctx_ref[hh], cvaug, preferred_element_type=F32)
        outs.append(tot[:, :LANES] * (1.0 / tot[:, LANES:]))
    o = jnp.where(lane_all < NA_HEAD_DIM, outs[0], outs[1])
    o_ref[...] = (o * _silu(gate_ref[...].astype(F32))).astype(BF16)


def _na_bias_tables(rpb):
    c = np.arange(GRID_W)
    c0 = np.clip(c - NA_WIN_C // 2, 0, GRID_W - NA_WIN_C)
    in_win = (c[None, :] >= c0[:, None]) & (c[None, :] < c0[:, None] + NA_WIN_C)
    period = 2 * GRID_W - 1
    lead = rpb.shape[:-1]
    w = jnp.concatenate([rpb[..., NA_WIN_C - 1:], jnp.zeros(lead + (period - (2 * NA_WIN_C - 1),), rpb.dtype),
                         rpb[..., :NA_WIN_C - 1]], axis=-1)
    flat = jnp.tile(w, (1,) * len(lead) + (GRID_W,))[..., :GRID_W * (period - 1)]
    toe = flat.reshape(lead + (GRID_W, period - 1))[..., :GRID_W]
    toe = jnp.where(in_win, toe.astype(F32) * LOG2E, -1e30)
    tabs = jnp.stack([toe[:, :, s:s + NA_WIN_R] for s in range(NA_WIN_R)], axis=1)
    tabs = jnp.transpose(tabs, (0, 1, 2, 4, 3, 5))
    return tabs.reshape(DEPTH, NA_WIN_R, NA_HEADS, GRID_W, NA_WIN_R * GRID_W)


def _neighbourhood_attention(p3, ck4, cv4, bias, l, *, rows_per_step=32):
    b, s, _ = p3.shape
    cq, ck, cv, cg = (_OFF[n] // LANES for n in ("nq", "nk", "nv", "n_gate"))
    n_ctx = ck4.shape[2]
    tq = rows_per_step * GRID_W
    n_loc = NA_WIN_R * GRID_W
    return pl.pallas_call(
        functools.partial(_na_attn_kernel, rows_per_step=rows_per_step),
        grid=(b, NA_HEADS // 2, GRID_H // rows_per_step),
        in_specs=[pl.BlockSpec((None, tq, LANES), lambda bi, hp, rb: (bi, rb, cq + hp)),
                  pl.BlockSpec((None, s, LANES), lambda bi, hp, rb: (bi, 0, ck + hp)),
                  pl.BlockSpec((None, s, LANES), lambda bi, hp, rb: (bi, 0, cv + hp)),
                  pl.BlockSpec((None, tq, LANES), lambda bi, hp, rb: (bi, rb, cg + hp)),
                  pl.BlockSpec((None, None, n_ctx, LANES), lambda bi, hp, rb: (bi, l, 0, hp)),
                  pl.BlockSpec((None, None, n_ctx, LANES), lambda bi, hp, rb: (bi, l, 0, hp)),
                  pl.BlockSpec((None, NA_WIN_R, 2, GRID_W, n_loc), lambda bi, hp, rb: (l, 0, hp, 0, 0))],
        out_specs=pl.BlockSpec((None, tq, LANES), lambda bi, hp, rb: (bi, rb, hp)),
        out_shape=jax.ShapeDtypeStruct((b, s, NA_HEADS * NA_HEAD_DIM), BF16),
        scratch_shapes=[pltpu.VMEM((2, tq, n_ctx), F32), pltpu.VMEM((2, tq, n_ctx), BF16),
                        pltpu.VMEM((2, 2 * GRID_W, n_loc), F32), pltpu.VMEM((s, 2 * LANES), BF16),
                        pltpu.VMEM((2, tq, 2 * LANES), F32)],
        compiler_params=_params(("arbitrary", "arbitrary", "arbitrary")),
        name="na_attn_latent",
    )(p3, p3, p3, p3, ck4, cv4, bias)


def _dft_tables():
    gd = FNET_GROUP_DIM
    n = np.arange(gd)
    ang = 2.0 * np.pi * ((n[:, None] * n[None, :]) % gd) / gd
    fc = np.concatenate([np.cos(ang), -np.sin(ang)], axis=1) / 16.0
    fl = np.concatenate([np.cos(ang), np.sin(ang)], axis=1) / 16.0
    npos = GRID_H * GRID_W
    k1 = np.arange(64)[None, :, None]
    col = np.arange(64)[:, None, None]
    row = np.arange(64)[None, None, :]
    a1 = 2.0 * np.pi * ((k1 * (64 * row + col)) % npos) / npos
    t1 = np.concatenate([np.cos(a1), np.sin(a1)], axis=2) / 8.0
    k = np.arange(64)
    a2 = 2.0 * np.pi * ((k[:, None] * k[None, :]) % 64) / 64.0
    t2 = np.concatenate([np.cos(a2), np.sin(a2)], axis=1) / 8.0
    cast = lambda a: jnp.asarray(a.astype(np.float32)).astype(BF16)
    return cast(fc), cast(fl), cast(t1), cast(t2)


def _fft_prompt_kernel(h_ref, fg_ref, fc_ref, fl_ref, o_ref):
    gd = FNET_GROUP_DIM
    for g in range(FNET_GROUPS):
        cols = slice(g * gd, (g + 1) * gd)
        z = jnp.dot(h_ref[:, cols], fc_ref[...], preferred_element_type=F32)
        stack = jnp.concatenate([z[:, :gd], z[:, gd:]], axis=0).astype(BF16)
        y = jnp.dot(fl_ref[...], stack, preferred_element_type=F32)
        o_ref[:, cols] = (y * _silu(fg_ref[:, cols].astype(F32))).astype(BF16)


FFT_PITCH = GRID_W + 4


def _fft_latent_kernel(h_ref, fg_ref, fc_ref, t1_ref, t2_ref, o_ref, z_ref, a_ref, y_ref):
    gd = FNET_GROUP_DIM
    rows = 512
    nt = 2 * gd // LANES
    pitch = FFT_PITCH
    for c in range(GRID_H * GRID_W // rows):
        z = jnp.dot(h_ref[c * rows:(c + 1) * rows, :], fc_ref[...], preferred_element_type=F32)
        for rl in range(rows // GRID_W):
            dst = (c * (rows // GRID_W) + rl) * pitch
            for t in range(nt):
                z_ref[t, dst:dst + GRID_W, :] = z[rl * GRID_W:(rl + 1) * GRID_W, t * LANES:(t + 1) * LANES]

    for col in range(GRID_W):
        tiles = [z_ref[t, pl.ds(col, GRID_H, stride=pitch), :] for t in range(nt)]
        zr = jnp.concatenate(tiles[:nt // 2], axis=1)
        zi = jnp.concatenate(tiles[nt // 2:], axis=1)
        stack = jnp.concatenate([jnp.concatenate([zr, zi], axis=1),
                                 jnp.concatenate([zi, -zr], axis=1)], axis=0).astype(BF16)
        a = jnp.dot(t1_ref[col], stack, preferred_element_type=F32)
        for t in range(nt):
            a_ref[t, col * pitch:col * pitch + GRID_H, :] = a[:, t * LANES:(t + 1) * LANES]

    for k1 in range(GRID_H):
        tiles = [a_ref[t, pl.ds(k1, GRID_W, stride=pitch), :] for t in range(nt)]
        stack = jnp.concatenate([jnp.concatenate(tiles[:nt // 2], axis=1),
                                 jnp.concatenate(tiles[nt // 2:], axis=1)], axis=0).astype(BF16)
        y = jnp.dot(t2_ref[...], stack, preferred_element_type=F32)
        for t in range(nt // 2):
            y_ref[t, pl.ds(k1, GRID_H, stride=pitch), :] = y[:, t * LANES:(t + 1) * LANES]

    for k2 in range(GRID_H):
        y = jnp.concatenate([y_ref[t, k2 * pitch:k2 * pitch + GRID_W, :] for t in range(nt // 2)], axis=1)
        sl = slice(k2 * GRID_W, (k2 + 1) * GRID_W)
        o_ref[sl, :] = (y * _silu(fg_ref[sl, :].astype(F32))).astype(BF16)


def _fourier_latent(h2d, p2d, fc, t1, t2, b):
    gd = FNET_GROUP_DIM
    n = GRID_H * GRID_W
    npad = GRID_H * FFT_PITCH
    cf = _OFF["f_gate"] // gd
    return pl.pallas_call(
        _fft_latent_kernel,
        grid=(b, FNET_GROUPS),
        in_specs=[pl.BlockSpec((n, gd), lambda bi, g: (bi, g)),
                  pl.BlockSpec((n, gd), lambda bi, g: (bi, cf + g)),
                  pl.BlockSpec((gd, 2 * gd), lambda bi, g: (0, 0)),
                  pl.BlockSpec((GRID_W, GRID_H, 2 * GRID_H), lambda bi, g: (0, 0, 0)),
                  pl.BlockSpec((GRID_H, 2 * GRID_W), lambda bi, g: (0, 0))],
        out_specs=pl.BlockSpec((n, gd), lambda bi, g: (bi, g)),
        out_shape=jax.ShapeDtypeStruct((b * n, D_MODEL), BF16),
        scratch_shapes=[pltpu.VMEM((2 * gd // LANES, npad, LANES), F32), pltpu.VMEM((2 * gd // LANES, npad, LANES), F32),
                        pltpu.VMEM((gd // LANES, npad, LANES), F32)],
        compiler_params=_params(("arbitrary", "arbitrary")),
        name="fourier_latent",
    )(h2d, p2d, fc, t1, t2)


def _merge_kernel(x_ref, mod_ref, mg_ref, u_ref, v_ref, sgate_ref, yd_ref, yn_ref, yf_ref,
                  gv_ref, sw_ref, sb_ref, psg_ref, pd_ref, pn_ref, pf_ref, wo_ref, gpost_ref,
                  o_ref, ysg_ref, *, latent, tm, tiles_per_batch):
    i = pl.program_id(0)
    lane = lax.broadcasted_iota(jnp.int32, (CHUNK, LANES), 1)
    for c in range(tm // CHUNK):
        sl = slice(c * CHUNK, (c + 1) * CHUNK)
        v = v_ref[sl, :].astype(F32)
        vn = (v * lax.rsqrt(jnp.mean(v * v, axis=-1, keepdims=True) + EPS) * gv_ref[...]).astype(BF16)
        parts = []
        for pr in range(SG_GROUPS // 2):
            vp = vn[:, pr * LANES:(pr + 1) * LANES]
            s0 = jnp.dot(sw_ref[2 * pr], vp, preferred_element_type=F32)
            s1 = jnp.dot(sw_ref[2 * pr + 1], vp, preferred_element_type=F32)
            parts.append(jnp.where(lane < SG_WIDTH // SG_GROUPS, s0, s1))
        s = jnp.concatenate(parts, axis=1) + sb_ref[...]
        y = u_ref[sl, :].astype(F32) * s * _silu(sgate_ref[sl, :].astype(F32))
        ysg_ref[sl, :] = y.astype(BF16)

    def gate(n):
        return jax.nn.sigmoid(mg_ref[:, n * D_MODEL:(n + 1) * D_MODEL].astype(F32))

    mixed = gate(0) * jnp.dot(ysg_ref[...], psg_ref[...], preferred_element_type=F32)
    mixed = mixed + gate(1) * jnp.dot(yd_ref[...], pd_ref[...], preferred_element_type=F32)
    mixed = mixed + gate(2) * jnp.dot(yn_ref[...], pn_ref[...], preferred_element_type=F32)
    mixed = mixed + gate(3) * jnp.dot(yf_ref[...], pf_ref[...], preferred_element_type=F32)
    out = jnp.dot(mixed.astype(BF16), wo_ref[...], preferred_element_type=F32)
    out = out * lax.rsqrt(jnp.mean(out * out, axis=-1, keepdims=True) + EPS) * gpost_ref[...]
    row = (1 + i // tiles_per_batch) if latent else 0
    g = mod_ref[pl.ds(row, 1), :][:, 2 * D_MODEL:]
    o_ref[...] = x_ref[...] + g * out


def _merge(x2d, mod_all, p2d, yd, yn, yf, w, l, *, latent, seq, tm=512):
    t = x2d.shape[0]
    cu, cv, cs = (_OFF[n] // SG_WIDTH for n in ("sg_u", "sg_v", "sg_gate"))
    tok = lambda width, c=0: pl.BlockSpec((tm, width), lambda i: (i, c))
    wspec = lambda *shape: pl.BlockSpec((None,) + shape, lambda i: (l,) + (0,) * len(shape))
    tiles_per_batch = seq // tm if latent else 1
    return pl.pallas_call(
        functools.partial(_merge_kernel, latent=latent, tm=tm, tiles_per_batch=tiles_per_batch),
        grid=(t // tm,),
        in_specs=[tok(D_MODEL), wspec(8, 3 * D_MODEL), tok(N_BRANCH * D_MODEL, 0),
                  tok(SG_WIDTH, cu), tok(SG_WIDTH, cv), tok(SG_WIDTH, cs),
                  tok(512), tok(512), tok(D_MODEL),
                  wspec(1, SG_WIDTH), wspec(SG_GROUPS, CHUNK, CHUNK), wspec(CHUNK, SG_WIDTH),
                  wspec(SG_WIDTH, D_MODEL), wspec(512, D_MODEL), wspec(512, D_MODEL),
                  wspec(D_MODEL, D_MODEL), wspec(D_MODEL, D_MODEL), wspec(1, D_MODEL)],
        out_specs=tok(D_MODEL),
        out_shape=jax.ShapeDtypeStruct((t, D_MODEL), F32),
        scratch_shapes=[pltpu.VMEM((tm, SG_WIDTH), BF16)],
        compiler_params=_params(("arbitrary",)),
        name="merge_latent" if latent else "merge_prompt",
    )(x2d, mod_all, p2d, p2d, p2d, p2d, yd, yn, yf,
      w["sg_norm_g"], w["sg_w"], w["sg_b"], w["p_sg"], w["p_diff"], w["p_na"], w["p_fnet"], w["w_out"],
      w["g_post"])


def _rope_tables(n_tokens):
    nf = DIFF_HEAD_DIM // 4
    t = np.arange(n_tokens)
    inv = ROPE_BASE ** (-np.arange(nf, dtype=np.float64) / nf)
    ang_r = (t // GRID_W).astype(np.float64)[:, None] * inv[None, :]
    ang_c = (t % GRID_W).astype(np.float64)[:, None] * inv[None, :]
    z = np.zeros_like(ang_r)
    cos64 = np.concatenate([np.cos(ang_r), np.cos(ang_r), np.cos(ang_c), np.cos(ang_c)], axis=1)
    sina64 = np.concatenate([-np.sin(ang_r), z, -np.sin(ang_c), z], axis=1)
    sinb64 = np.concatenate([z, np.sin(ang_r), z, np.sin(ang_c)], axis=1)
    two = lambda a: jnp.asarray(np.concatenate([a, a], axis=1).astype(np.float32))
    return two(cos64), two(sina64), two(sinb64)


def kernel(x_prompt, x_sample, cache_diff_k, cache_diff_v, cache_na_k, cache_na_v, c, c_ctx,
           w_mod, b_mod, g_pre, g_post, w_in, sg_norm_g, sg_w, sg_b,
           diff_lam_q1, diff_lam_k1, diff_lam_q2, diff_lam_k2, diff_subln_g, na_rpb,
           w_proj_sg, w_proj_diff, w_proj_na, w_proj_fnet, w_out):
    bp, sp, _ = x_prompt.shape
    bs, ss, _ = x_sample.shape
    n_ctx = cache_diff_k.shape[2]

    cond8 = jnp.zeros((8, D_MODEL), F32).at[0].set(c_ctx).at[1:1 + bs].set(c)
    mod_all = _modulation(cond8, w_mod, b_mod)

    w_in_b = w_in.astype(BF16)
    weights = {
        "sg_norm_g": sg_norm_g.reshape(DEPTH, 1, SG_WIDTH),
        "sg_w": sg_w.astype(BF16),
        "sg_b": jnp.repeat(jnp.transpose(sg_b, (0, 2, 1)), SG_WIDTH // SG_GROUPS, axis=2),
        "p_sg": w_proj_sg.astype(BF16), "p_diff": w_proj_diff.astype(BF16),
        "p_na": w_proj_na.astype(BF16), "p_fnet": w_proj_fnet.astype(BF16),
        "w_out": w_out.astype(BF16), "g_post": g_post.reshape(DEPTH, 1, D_MODEL),
    }
    g_pre3 = g_pre.reshape(DEPTH, 1, D_MODEL)
    lamp = jnp.stack([diff_lam_q1, diff_lam_k1, diff_lam_q2, diff_lam_k2], axis=1)
    subg3 = diff_subln_g.reshape(DEPTH, 1, 2 * DIFF_HEAD_DIM)
    rope_tabs = _rope_tables(ss)
    fc, fl, t1, t2 = _dft_tables()
    bias = _na_bias_tables(na_rpb)
    ckd = cache_diff_k.reshape(bs, DEPTH, n_ctx, DIFF_HEADS * 2 * DIFF_HEAD_DIM)
    cvd = cache_diff_v.reshape(bs, DEPTH, n_ctx, DIFF_HEADS * 2 * DIFF_HEAD_DIM)
    ckn = cache_na_k.reshape(bs, DEPTH, n_ctx, NA_HEADS * NA_HEAD_DIM)
    cvn = cache_na_v.reshape(bs, DEPTH, n_ctx, NA_HEADS * NA_HEAD_DIM)

    xp = x_prompt.reshape(bp * sp, D_MODEL)
    xs = x_sample.reshape(bs * ss, D_MODEL)
    caches = None
    for l in range(DEPTH):
        pp, hp, *caches = _inproj(xp, mod_all, g_pre3, w_in_b, l, latent=False, tm=4 * sp, seq=sp, n_col_tiles=7,
                                  cache_bufs=caches)
        pp3 = pp.reshape(bp, sp, IN_WIDTH)
        yd, yn, yf = (y.reshape(bp * sp, -1) for y in _prompt_mixers(pp3, hp, lamp, subg3, fc, fl, l))
        xp = _merge(xp, mod_all, pp, yd, yn, yf, weights, l, latent=False, seq=sp)
        ps, hs = _inproj(xs, mod_all, g_pre3, w_in_b, l, latent=True, tm=1024, seq=ss, n_col_tiles=3,
                         rope_tabs=rope_tabs)
        ps3 = ps.reshape(bs, ss, IN_WIDTH)
        yd = _diff_attention(ps3, lamp, subg3, l, tq=2048, heads=1, sub=8, ctx=(ckd, cvd)).reshape(bs * ss, -1)
        yn = _neighbourhood_attention(ps3, ckn, cvn, bias, l).reshape(bs * ss, -1)
        yf = _fourier_latent(hs, ps, fc, t1, t2, bs)
        xs = _merge(xs, mod_all, ps, yd, yn, yf, weights, l, latent=True, seq=ss)

    dk5, dv5, nk5, nv5 = caches
    return (xp.reshape(bp, sp, D_MODEL), xs.reshape(bs, ss, D_MODEL),
            dk5, dv5,
            nk5.reshape(bp, DEPTH, sp, NA_HEADS, NA_HEAD_DIM),
            nv5.reshape(bp, DEPTH, sp, NA_HEADS, NA_HEAD_DIM))
```

```python
import functools
import math

import numpy as np
import jax
import jax.numpy as jnp
from jax import lax
from jax.experimental import pallas as pl
from jax.experimental.pallas import tpu as pltpu

F32 = jnp.float32
BF16 = jnp.bfloat16

D_MODEL = 1024
DEPTH = 4
GRID_W = 64
GRID_H = 64
EPS = 1e-6
ROPE_BASE = 10000.0
CHUNK = 128
SG_WIDTH = 512
SG_GROUPS = 8
DIFF_HEADS = 4
DIFF_HEAD_DIM = 64
NA_HEADS = 8
NA_HEAD_DIM = 64
NA_WIN_R = 8
NA_WIN_C = 16
FNET_GROUPS = 4
FNET_GROUP_DIM = 256
N_BRANCH = 4
LANES = 128
VMEM_LIMIT = 56 * 1024 * 1024
LOG2E = math.log2(math.e)

_REF_SEGMENTS = (("sg_u", 512), ("sg_v", 512), ("sg_gate", 512),
                 ("dq", 512), ("dk", 512), ("dv", 512), ("d_gate", 512),
                 ("nq", 512), ("nk", 512), ("nv", 512), ("n_gate", 512),
                 ("f_gate", 1024), ("merge", 4096))
_MY_ORDER = ("merge", "f_gate", "sg_u", "sg_v", "sg_gate", "dq", "dk", "dv", "d_gate",
             "nq", "nk", "nv", "n_gate")
UNIT = 512


def _layout():
    ref_off, o = {}, 0
    for name, width in _REF_SEGMENTS:
        ref_off[name] = (o, width)
        o += width
    my_off, o, unit_names = {}, 0, []
    for name in _MY_ORDER:
        my_off[name] = o
        o += ref_off[name][1]
        unit_names += [name] * (ref_off[name][1] // UNIT)
    return ref_off, my_off, o, tuple(unit_names)


_REF_OFF, _OFF, IN_WIDTH, _UNIT_NAMES = _layout()


def _ref_unit(n):
    merge_units = _REF_OFF["merge"][1] // UNIT
    gate_units = _REF_OFF["f_gate"][1] // UNIT
    return jnp.where(n < merge_units, n + _REF_OFF["merge"][0] // UNIT,
                     jnp.where(n < merge_units + gate_units, n - merge_units + _REF_OFF["f_gate"][0] // UNIT,
                               n - merge_units - gate_units))


def _silu(x):
    return x * jax.nn.sigmoid(x)


def _params(sem):
    return pltpu.CompilerParams(dimension_semantics=sem, vmem_limit_bytes=VMEM_LIMIT)


def _mod_kernel(cond_ref, w_ref, b_ref, o_ref):
    s = _silu(cond_ref[...])
    o_ref[...] = jnp.dot(s, w_ref[...], precision=lax.Precision.HIGHEST,
                         preferred_element_type=F32) + b_ref[...]


def _modulation(cond8, w_mod, b_mod):
    b3 = b_mod.reshape(DEPTH, 1, 3 * D_MODEL)
    return pl.pallas_call(
        _mod_kernel,
        grid=(DEPTH, 3),
        in_specs=[pl.BlockSpec((8, D_MODEL), lambda l, j: (0, 0)),
                  pl.BlockSpec((None, D_MODEL, D_MODEL), lambda l, j: (l, 0, j)),
                  pl.BlockSpec((None, 1, D_MODEL), lambda l, j: (l, 0, j))],
        out_specs=pl.BlockSpec((None, 8, D_MODEL), lambda l, j: (l, 0, j)),
        out_shape=jax.ShapeDtypeStruct((DEPTH, 8, 3 * D_MODEL), F32),
        compiler_params=_params(("arbitrary", "arbitrary")),
        name="modulation",
    )(cond8, w_mod, b3)


def _rope(acc, cos_ref, sina_ref, sinb_ref):
    cos, sina, sinb = cos_ref[...], sina_ref[...], sinb_ref[...]
    outs = []
    for hd in range(acc.shape[1] // LANES):
        a = acc[:, hd * LANES:(hd + 1) * LANES]
        up = pltpu.roll(a, LANES - 16, 1)
        dn = pltpu.roll(a, 16, 1)
        outs.append(a * cos + up * sina + dn * sinb)
    return jnp.concatenate(outs, axis=1)


def _inproj_kernel(*refs, latent, tm, seq, tiles_per_batch, n_alias, n_col_tiles):
    units_per_tile = len(_UNIT_NAMES) // n_col_tiles
    (x_ref, mod_ref, gpre_ref) = refs[:3]
    w_refs = refs[3:3 + units_per_tile]
    rest = refs[3 + units_per_tile:]
    if latent:
        (cos_ref, sina_ref, sinb_ref, p_ref, h_ref) = rest
        cache_refs = {}
    else:
        (p_ref, h_ref, dk_ref, dv_ref, nk_ref, nv_ref) = rest[n_alias:]
        cache_refs = {"dk": dk_ref, "dv": dv_ref, "nk": nk_ref, "nv": nv_ref}
    i = pl.program_id(0)
    j = pl.program_id(1)

    @pl.when(j == 0)
    def _():
        x = x_ref[...]
        r = lax.rsqrt(jnp.mean(x * x, axis=-1, keepdims=True) + EPS)
        row = (1 + i // tiles_per_batch) if latent else 0
        m = mod_ref[pl.ds(row, 1), :]
        shift = m[:, :D_MODEL]
        scale = m[:, D_MODEL:2 * D_MODEL]
        h_ref[...] = ((x * r * gpre_ref[...]) * (1.0 + scale) + shift).astype(BF16)

    def unit(u, name):
        cols = slice(u * UNIT, (u + 1) * UNIT)
        acc = jnp.dot(h_ref[...], w_refs[u][...], preferred_element_type=F32)
        if name in ("dk", "dv") and not latent:
            for hd in range(DIFF_HEADS):
                cache_refs[name][:, :, hd, :] = acc[:, hd * LANES:(hd + 1) * LANES].reshape(tm // seq, seq, LANES)
        elif name in cache_refs:
            cache_refs[name][...] = acc.reshape(tm // seq, seq, UNIT)
        if name in ("dq", "nq"):
            acc = acc * (DIFF_HEAD_DIM ** -0.5 * LOG2E)
        if latent and name in ("dq", "dk"):
            acc = _rope(acc, cos_ref, sina_ref, sinb_ref)
        p_ref[:, cols] = acc.astype(BF16)

    for jt in range(n_col_tiles):
        @pl.when(j == jt)
        def _(jt=jt):
            for u in range(units_per_tile):
                unit(u, _UNIT_NAMES[jt * units_per_tile + u])


def _inproj(x2d, mod_all, g_pre3, w_in_b, l, *, latent, tm, seq, n_col_tiles, rope_tabs=None, cache_bufs=None):
    t = x2d.shape[0]
    n_i = t // tm
    tn = IN_WIDTH // n_col_tiles
    units_per_tile = tn // UNIT
    in_specs = [pl.BlockSpec((tm, D_MODEL), lambda i, j: (i, 0)),
                pl.BlockSpec((None, 8, 3 * D_MODEL), lambda i, j: (l, 0, 0)),
                pl.BlockSpec((None, 1, D_MODEL), lambda i, j: (l, 0, 0))]
    args = [x2d, mod_all, g_pre3]
    for u in range(units_per_tile):
        in_specs.append(pl.BlockSpec((None, D_MODEL, UNIT),
                                     lambda i, j, u=u: (l, 0, _ref_unit(j * units_per_tile + u))))
        args.append(w_in_b)
    out_specs = [pl.BlockSpec((tm, tn), lambda i, j: (i, j)),
                 pl.BlockSpec((tm, D_MODEL), lambda i, j: (i, 0))]
    out_shape = [jax.ShapeDtypeStruct((t, IN_WIDTH), BF16),
                 jax.ShapeDtypeStruct((t, D_MODEL), BF16)]
    tiles_per_batch = 1
    aliases = {}
    n_alias = 0
    if latent:
        tiles_per_batch = seq // tm
        for tab in rope_tabs:
            in_specs.append(pl.BlockSpec((tm, LANES), lambda i, j: (i % tiles_per_batch, 0)))
            args.append(tab)
    else:
        nb = tm // seq
        if cache_bufs is not None:
            n_alias = len(cache_bufs)
            for k, buf in enumerate(cache_bufs):
                in_specs.append(pl.BlockSpec(memory_space=pl.ANY))
                args.append(buf)
                aliases[3 + units_per_tile + k] = 2 + k
        for _ in range(2):
            out_specs.append(pl.BlockSpec((nb, None, seq, DIFF_HEADS, LANES), lambda i, j: (i, l, 0, 0, 0)))
            out_shape.append(jax.ShapeDtypeStruct((t // seq, DEPTH, seq, DIFF_HEADS, LANES), F32))
        for _ in range(2):
            out_specs.append(pl.BlockSpec((nb, None, seq, UNIT), lambda i, j: (i, l, 0, 0)))
            out_shape.append(jax.ShapeDtypeStruct((t // seq, DEPTH, seq, UNIT), F32))
    return pl.pallas_call(
        functools.partial(_inproj_kernel, latent=latent, tm=tm, seq=seq, tiles_per_batch=tiles_per_batch,
                          n_alias=n_alias, n_col_tiles=n_col_tiles),
        grid=(n_i, n_col_tiles),
        in_specs=in_specs,
        out_specs=out_specs,
        out_shape=out_shape,
        input_output_aliases=aliases,
        compiler_params=_params(("arbitrary", "arbitrary")),
        name="inproj_latent" if latent else "inproj_prompt",
    )(*args)


def _lane_tile_reduce(op, x):
    parts = [x[:, t * LANES:(t + 1) * LANES] for t in range(x.shape[1] // LANES)]
    return functools.reduce(op, parts)


def _qk_chunk(q, k_ref, s_ref, kcols, c, kc):
    rows = slice(c * kc, (c + 1) * kc)
    s = lax.dot_general(q, k_ref[rows, kcols], (((1,), (1,)), ((), ())), preferred_element_type=F32)
    s_ref[:, rows] = s
    return _lane_tile_reduce(jnp.maximum, s)


def _pv_chunk(s_ref, m, vaug, c, kc):
    rows = slice(c * kc, (c + 1) * kc)
    e = jnp.exp2(s_ref[:, rows] - m).astype(BF16)
    return jnp.dot(e, vaug[rows], preferred_element_type=F32)


def _pipelined_attention(streams, k_ref, s_refs, n_keys, kc):
    nck = n_keys // kc
    acc = lambda a, b: b if a is None else a + b
    mx = lambda a, b: b if a is None else jnp.maximum(a, b)
    outs = []
    if nck == 1:
        for q, cols, vaug in streams:
            s = lax.dot_general(q, k_ref[:, cols], (((1,), (1,)), ((), ())), preferred_element_type=F32)
            e = jnp.exp2(s - jnp.max(s, axis=-1, keepdims=True)).astype(BF16)
            r = jnp.dot(e, vaug, preferred_element_type=F32)
            outs.append(r[:, :LANES] * (1.0 / r[:, LANES:]))
        return outs
    mp = None
    for c in range(nck):
        mp = mx(mp, _qk_chunk(streams[0][0], k_ref, s_refs[0], streams[0][1], c, kc))
    for i, (_, _, vaug) in enumerate(streams):
        m = jnp.max(mp, axis=-1, keepdims=True)
        r = mp = None
        for c in range(nck):
            r = acc(r, _pv_chunk(s_refs[i % 2], m, vaug, c, kc))
            if i + 1 < len(streams):
                q_n, cols_n, _ = streams[i + 1]
                mp = mx(mp, _qk_chunk(q_n, k_ref, s_refs[(i + 1) % 2], cols_n, c, kc))
        outs.append(r[:, :LANES] * (1.0 / r[:, LANES:]))
    return outs


def _diff_attn_kernel(*refs, lam_init, n_self, has_ctx, heads, kc, sub):
    if has_ctx:
        (lamp_ref, subg_ref, q_ref, k_ref, v_ref, gate_ref, ck_ref, cv_ref, o_ref,
         s1_ref, s2_ref, kc_ref, vc_ref) = refs

        @pl.when(pl.program_id(2) == 0)
        def _():
            kc_ref[0:n_self, :] = k_ref[...]
            kc_ref[n_self:, :] = ck_ref[...].astype(BF16)
            vc_ref[0:n_self, :LANES] = v_ref[...]
            vc_ref[n_self:, :LANES] = cv_ref[...].astype(BF16)
            vc_ref[:, LANES:] = jnp.ones((vc_ref.shape[0], LANES), BF16)

        k_ref = kc_ref
    else:
        (lamp_ref, subg_ref, q_ref, k_ref, v_ref, gate_ref, o_ref, s1_ref, s2_ref) = refs
    n_keys = k_ref.shape[0]
    tq = q_ref.shape[0] // sub
    lp = lamp_ref[...]
    lam = (jnp.exp(jnp.sum(lp[0:1] * lp[1:2], axis=-1, keepdims=True))
           - jnp.exp(jnp.sum(lp[2:3] * lp[3:4], axis=-1, keepdims=True)) + lam_init)
    lane = lax.broadcasted_iota(jnp.int32, (tq, LANES), 1)
    zero = jnp.zeros((tq, LANES), BF16)
    streams, where = [], []
    for hd in range(heads):
        cols = slice(hd * LANES, (hd + 1) * LANES)
        if has_ctx:
            vaug = vc_ref[...]
        else:
            vaug = jnp.concatenate([v_ref[:, cols], jnp.ones((n_keys, LANES), BF16)], axis=1)
        for sb in range(sub):
            rows = slice(sb * tq, (sb + 1) * tq)
            q = q_ref[rows, cols]
            streams.append((jnp.where(lane < DIFF_HEAD_DIM, q, zero), cols, vaug))
            streams.append((jnp.where(lane >= DIFF_HEAD_DIM, q, zero), cols, vaug))
            where.append((rows, cols))
    outs = _pipelined_attention(streams, k_ref, (s1_ref, s2_ref), n_keys, kc)
    for n, (rows, cols) in enumerate(where):
        o = outs[2 * n] - lam * outs[2 * n + 1]
        y = o * lax.rsqrt(jnp.mean(o * o, axis=-1, keepdims=True) + EPS) * subg_ref[...]
        y = y * (1.0 - lam_init)
        o_ref[rows, cols] = (y * _silu(gate_ref[rows, cols].astype(F32))).astype(BF16)


def _diff_attention(p3, lamp, subg3, l, *, tq, heads, ctx=None, sub=1):
    b, s, _ = p3.shape
    w = heads * LANES
    cq, ck, cv, cg = (_OFF[n] // w for n in ("dq", "dk", "dv", "d_gate"))
    lam_init = 0.8 - 0.6 * math.exp(-0.3 * l)
    in_specs = [pl.BlockSpec((None, 4, DIFF_HEAD_DIM), lambda bi, h, qi: (l, 0, 0)),
                pl.BlockSpec((None, 1, LANES), lambda bi, h, qi: (l, 0, 0)),
                pl.BlockSpec((None, tq, w), lambda bi, h, qi: (bi, qi, cq + h)),
                pl.BlockSpec((None, s, w), lambda bi, h, qi: (bi, 0, ck + h)),
                pl.BlockSpec((None, s, w), lambda bi, h, qi: (bi, 0, cv + h)),
                pl.BlockSpec((None, tq, w), lambda bi, h, qi: (bi, qi, cg + h))]
    args = [lamp, subg3, p3, p3, p3, p3]
    n_keys = s
    scratch = []
    if ctx is not None:
        ck4, cv4 = ctx
        n_ctx = ck4.shape[2]
        n_keys = s + n_ctx
        for arr in (ck4, cv4):
            in_specs.append(pl.BlockSpec((None, None, n_ctx, w), lambda bi, h, qi: (bi, l, 0, h)))
            args.append(arr)
        scratch = [pltpu.VMEM((n_keys, w), BF16), pltpu.VMEM((n_keys, 2 * w), BF16)]
    kc = min(n_keys, 512)
    scratch = [pltpu.VMEM((tq // sub, n_keys), F32), pltpu.VMEM((tq // sub, n_keys), F32)] + scratch
    return pl.pallas_call(
        functools.partial(_diff_attn_kernel, lam_init=lam_init, n_self=s, has_ctx=ctx is not None,
                          heads=heads, kc=kc, sub=sub),
        grid=(b, DIFF_HEADS // heads, s // tq),
        in_specs=in_specs,
        out_specs=pl.BlockSpec((None, tq, w), lambda bi, h, qi: (bi, qi, h)),
        out_shape=jax.ShapeDtypeStruct((b, s, DIFF_HEADS * LANES), BF16),
        scratch_shapes=scratch,
        compiler_params=_params(("arbitrary", "arbitrary", "arbitrary")),
        name="diff_attn_latent" if ctx is not None else "diff_attn_prompt",
    )(*args)


def _ctx_attn_kernel(q_ref, k_ref, v_ref, gate_ref, o_ref):
    dn = (((1,), (1,)), ((), ()))
    n = k_ref.shape[0]
    ones = jnp.ones((n, LANES), BF16)
    for hp in range(NA_HEADS // 2):
        cols = slice(hp * LANES, (hp + 1) * LANES)
        q = q_ref[:, cols]
        lane = lax.broadcasted_iota(jnp.int32, q.shape, 1)
        zero = jnp.zeros_like(q)
        qs = jnp.concatenate([jnp.where(lane < NA_HEAD_DIM, q, zero), jnp.where(lane >= NA_HEAD_DIM, q, zero)], axis=0)
        s = lax.dot_general(qs, k_ref[:, cols], dn, preferred_element_type=F32)
        e = jnp.exp2(s - jnp.max(s, axis=-1, keepdims=True)).astype(BF16)
        r = jnp.dot(e, jnp.concatenate([v_ref[:, cols], ones], axis=1), preferred_element_type=F32)
        o2 = r[:, :LANES] * (1.0 / r[:, LANES:])
        o = jnp.where(lane < NA_HEAD_DIM, o2[:n], o2[n:])
        o_ref[:, cols] = (o * _silu(gate_ref[:, cols].astype(F32))).astype(BF16)


def _prompt_mixers_kernel(lamp_ref, subg_ref, dq_ref, dk_ref, dv_ref, dg_ref, nq_ref, nk_ref, nv_ref, ng_ref,
                          h_ref, fg_ref, fc_ref, fl_ref, yd_ref, yn_ref, yf_ref, *, lam_init):
    _diff_attn_kernel(lamp_ref, subg_ref, dq_ref, dk_ref, dv_ref, dg_ref, yd_ref, None, None,
                      lam_init=lam_init, n_self=dk_ref.shape[0], has_ctx=False, heads=DIFF_HEADS,
                      kc=dk_ref.shape[0], sub=1)
    _ctx_attn_kernel(nq_ref, nk_ref, nv_ref, ng_ref, yn_ref)
    _fft_prompt_kernel(h_ref, fg_ref, fc_ref, fl_ref, yf_ref)


def _prompt_mixers(p3, h2d, lamp, subg3, fc, fl, l):
    b, s, _ = p3.shape
    w = DIFF_HEADS * LANES
    seg = lambda name: pl.BlockSpec((None, s, w), lambda bi: (bi, 0, _OFF[name] // w))
    tok = lambda width: pl.BlockSpec((None, s, width), lambda bi: (bi, 0, 0))
    lam_init = 0.8 - 0.6 * math.exp(-0.3 * l)
    return pl.pallas_call(
        functools.partial(_prompt_mixers_kernel, lam_init=lam_init),
        grid=(b,),
        in_specs=[pl.BlockSpec((None, 4, DIFF_HEAD_DIM), lambda bi: (l, 0, 0)),
                  pl.BlockSpec((None, 1, LANES), lambda bi: (l, 0, 0))]
                 + [seg(n) for n in ("dq", "dk", "dv", "d_gate", "nq", "nk", "nv", "n_gate")]
                 + [tok(D_MODEL),
                    pl.BlockSpec((None, s, D_MODEL), lambda bi: (bi, 0, _OFF["f_gate"] // D_MODEL)),
                    pl.BlockSpec((FNET_GROUP_DIM, 2 * FNET_GROUP_DIM), lambda bi: (0, 0)),
                    pl.BlockSpec((s, 2 * s), lambda bi: (0, 0))],
        out_specs=[tok(w), tok(w), tok(D_MODEL)],
        out_shape=[jax.ShapeDtypeStruct((b, s, w), BF16), jax.ShapeDtypeStruct((b, s, w), BF16),
                   jax.ShapeDtypeStruct((b, s, D_MODEL), BF16)],
        compiler_params=_params(("arbitrary",)),
        name="prompt_mixers",
    )(lamp, subg3, *([p3] * 8), h2d.reshape(b, s, D_MODEL), p3, fc, fl)


def _na_attn_kernel(q_ref, k_ref, v_ref, gate_ref, ck_ref, cv_ref, bias_ref, o_ref,
                    sctx_ref, pctx_ref, sl_ref, vaug_ref, acc_ref, *, rows_per_step):
    rb = pl.program_id(2)
    n_loc = NA_WIN_R * GRID_W
    ones = jnp.ones((n_loc, LANES), BF16)

    @pl.when(rb == 0)
    def _():
        vaug_ref[:, :LANES] = v_ref[...]
        vaug_ref[:, LANES:] = jnp.ones((vaug_ref.shape[0], LANES), BF16)

    ckb = ck_ref[...].astype(BF16)
    cvaug = jnp.concatenate([cv_ref[...].astype(BF16), ones], axis=1)
    dn = (((1,), (1,)), ((), ()))
    q_all = q_ref[...]
    lane_all = lax.broadcasted_iota(jnp.int32, q_all.shape, 1)
    zero_all = jnp.zeros_like(q_all)
    for hh in range(2):
        sel = (lane_all < NA_HEAD_DIM) if hh == 0 else (lane_all >= NA_HEAD_DIM)
        sctx_ref[hh] = lax.dot_general(jnp.where(sel, q_all, zero_all), ckb, dn, preferred_element_type=F32)
    lane = lax.broadcasted_iota(jnp.int32, (GRID_W, LANES), 1)

    def ctx_scores(rl):
        rows = slice(rl * GRID_W, (rl + 1) * GRID_W)
        return jnp.concatenate([sctx_ref[0, rows, :], sctx_ref[1, rows, :]], axis=0)

    def score_stage(rl):
        r = rb * rows_per_step + rl
        r0 = jnp.clip(r - NA_WIN_R // 2, 0, GRID_H - NA_WIN_R)
        start = r0 - r + (NA_WIN_R - 1)
        koff = pl.multiple_of(r0 * GRID_W, GRID_W)
        q = q_ref[rl * GRID_W:(rl + 1) * GRID_W, :]
        zero = jnp.zeros_like(q)
        qs = jnp.concatenate([jnp.where(lane < NA_HEAD_DIM, q, zero),
                              jnp.where(lane >= NA_HEAD_DIM, q, zero)], axis=0)
        bias = jnp.concatenate(
            [jnp.concatenate([bias_ref[hh, start + 2 * jj] for jj in range(NA_WIN_R // 2)], axis=1) for hh in range(2)],
            axis=0)
        s = lax.dot_general(qs, k_ref[pl.ds(koff, n_loc), :], dn, preferred_element_type=F32) + bias
        sl_ref[rl % 2] = s
        mp = jnp.maximum(_lane_tile_reduce(jnp.maximum, s), _lane_tile_reduce(jnp.maximum, ctx_scores(rl)))
        return jnp.max(mp, axis=-1, keepdims=True), koff

    def value_stage(rl, m, koff):
        rows = slice(rl * GRID_W, (rl + 1) * GRID_W)
        e_ctx = jnp.exp2(ctx_scores(rl) - m).astype(BF16)
        pctx_ref[0, rows, :] = e_ctx[:GRID_W]
        pctx_ref[1, rows, :] = e_ctx[GRID_W:]
        e_loc = jnp.exp2(sl_ref[rl % 2] - m).astype(BF16)
        res = jnp.dot(e_loc, vaug_ref[pl.ds(koff, n_loc), :], preferred_element_type=F32)
        acc_ref[0, rows, :] = res[:GRID_W]
        acc_ref[1, rows, :] = res[GRID_W:]

    state = score_stage(0)
    for rl in range(rows_per_step):
        nxt = score_stage(rl + 1) if rl + 1 < rows_per_step else None
        value_stage(rl, *state)
        state = nxt
    outs = []
    for hh in range(2):
        tot = acc_ref[hh] + jnp.dot(pctx_ref[hh], cvaug, preferred_element_type=F32)
        outs.append(tot[:, :LANES] * (1.0 / tot[:, LANES:]))
    o = jnp.where(lane_all < NA_HEAD_DIM, outs[0], outs[1])
    o_ref[...] = (o * _silu(gate_ref[...].astype(F32))).astype(BF16)


def _na_bias_tables(rpb):
    c = np.arange(GRID_W)
    c0 = np.clip(c - NA_WIN_C // 2, 0, GRID_W - NA_WIN_C)
    in_win = (c[None, :] >= c0[:, None]) & (c[None, :] < c0[:, None] + NA_WIN_C)
    dc = np.clip(c[None, :] - c[:, None], -(NA_WIN_C - 1), NA_WIN_C - 1) + NA_WIN_C - 1
    onehot = ((np.arange(2 * NA_WIN_C - 1)[:, None, None] == dc[None]) & in_win[None]).astype(np.float32)
    neg = np.where(in_win, 0.0, -1e30).astype(np.float32)
    toe = jnp.einsum("lhdc,cqk->lhdqk", rpb.astype(F32) * LOG2E, jnp.asarray(onehot),
                     precision=lax.Precision.HIGHEST) + jnp.asarray(neg)
    return jnp.concatenate([toe[:, :, :-1], toe[:, :, 1:]], axis=-1)


def _neighbourhood_attention(p3, ck4, cv4, bias, l, *, rows_per_step=32):
    b, s, _ = p3.shape
    cq, ck, cv, cg = (_OFF[n] // LANES for n in ("nq", "nk", "nv", "n_gate"))
    n_ctx = ck4.shape[2]
    tq = rows_per_step * GRID_W
    n_loc = NA_WIN_R * GRID_W
    return pl.pallas_call(
        functools.partial(_na_attn_kernel, rows_per_step=rows_per_step),
        grid=(b, NA_HEADS // 2, GRID_H // rows_per_step),
        in_specs=[pl.BlockSpec((None, tq, LANES), lambda bi, hp, rb: (bi, rb, cq + hp)),
                  pl.BlockSpec((None, s, LANES), lambda bi, hp, rb: (bi, 0, ck + hp)),
                  pl.BlockSpec((None, s, LANES), lambda bi, hp, rb: (bi, 0, cv + hp)),
                  pl.BlockSpec((None, tq, LANES), lambda bi, hp, rb: (bi, rb, cg + hp)),
                  pl.BlockSpec((None, None, n_ctx, LANES), lambda bi, hp, rb: (bi, l, 0, hp)),
                  pl.BlockSpec((None, None, n_ctx, LANES), lambda bi, hp, rb: (bi, l, 0, hp)),
                  pl.BlockSpec((None, 2, 2 * NA_WIN_R - 2, GRID_W, 2 * GRID_W), lambda bi, hp, rb: (l, hp, 0, 0, 0))],
        out_specs=pl.BlockSpec((None, tq, LANES), lambda bi, hp, rb: (bi, rb, hp)),
        out_shape=jax.ShapeDtypeStruct((b, s, NA_HEADS * NA_HEAD_DIM), BF16),
        scratch_shapes=[pltpu.VMEM((2, tq, n_ctx), F32), pltpu.VMEM((2, tq, n_ctx), BF16),
                        pltpu.VMEM((2, 2 * GRID_W, n_loc), F32), pltpu.VMEM((s, 2 * LANES), BF16),
                        pltpu.VMEM((2, tq, 2 * LANES), F32)],
        compiler_params=_params(("arbitrary", "arbitrary", "arbitrary")),
        name="na_attn_latent",
    )(p3, p3, p3, p3, ck4, cv4, bias)


def _dft_tables():
    gd = FNET_GROUP_DIM
    n = np.arange(gd)
    ang = 2.0 * np.pi * ((n[:, None] * n[None, :]) % gd) / gd
    fc = np.concatenate([np.cos(ang), -np.sin(ang)], axis=1) / 16.0
    fl = np.concatenate([np.cos(ang), np.sin(ang)], axis=1) / 16.0
    npos = GRID_H * GRID_W
    k1 = np.arange(64)[None, :, None]
    col = np.arange(64)[:, None, None]
    row = np.arange(64)[None, None, :]
    a1 = 2.0 * np.pi * ((k1 * (64 * row + col)) % npos) / npos
    t1 = np.concatenate([np.cos(a1), np.sin(a1)], axis=2) / 8.0
    k = np.arange(64)
    a2 = 2.0 * np.pi * ((k[:, None] * k[None, :]) % 64) / 64.0
    t2 = np.concatenate([np.cos(a2), np.sin(a2)], axis=1) / 8.0
    cast = lambda a: jnp.asarray(a.astype(np.float32)).astype(BF16)
    return cast(fc), cast(fl), cast(t1), cast(t2)


def _fft_prompt_kernel(h_ref, fg_ref, fc_ref, fl_ref, o_ref):
    gd = FNET_GROUP_DIM
    for g in range(FNET_GROUPS):
        cols = slice(g * gd, (g + 1) * gd)
        z = jnp.dot(h_ref[:, cols], fc_ref[...], preferred_element_type=F32)
        stack = jnp.concatenate([z[:, :gd], z[:, gd:]], axis=0).astype(BF16)
        y = jnp.dot(fl_ref[...], stack, preferred_element_type=F32)
        o_ref[:, cols] = (y * _silu(fg_ref[:, cols].astype(F32))).astype(BF16)


FFT_PITCH = GRID_W + 4


def _fft_latent_kernel(h_ref, fg_ref, fc_ref, t1_ref, t2_ref, o_ref, z_ref, a_ref, y_ref):
    gd = FNET_GROUP_DIM
    rows = 512
    nt = 2 * gd // LANES
    pitch = FFT_PITCH
    for c in range(GRID_H * GRID_W // rows):
        z = jnp.dot(h_ref[c * rows:(c + 1) * rows, :], fc_ref[...], preferred_element_type=F32)
        for rl in range(rows // GRID_W):
            dst = (c * (rows // GRID_W) + rl) * pitch
            for t in range(nt):
                z_ref[t, dst:dst + GRID_W, :] = z[rl * GRID_W:(rl + 1) * GRID_W, t * LANES:(t + 1) * LANES]

    for col in range(GRID_W):
        tiles = [z_ref[t, pl.ds(col, GRID_H, stride=pitch), :] for t in range(nt)]
        zr = jnp.concatenate(tiles[:nt // 2], axis=1)
        zi = jnp.concatenate(tiles[nt // 2:], axis=1)
        stack = jnp.concatenate([jnp.concatenate([zr, zi], axis=1),
                                 jnp.concatenate([zi, -zr], axis=1)], axis=0).astype(BF16)
        a = jnp.dot(t1_ref[col], stack, preferred_element_type=F32)
        for t in range(nt):
            a_ref[t, col * pitch:col * pitch + GRID_H, :] = a[:, t * LANES:(t + 1) * LANES]

    for k1 in range(GRID_H):
        tiles = [a_ref[t, pl.ds(k1, GRID_W, stride=pitch), :] for t in range(nt)]
        stack = jnp.concatenate([jnp.concatenate(tiles[:nt // 2], axis=1),
                                 jnp.concatenate(tiles[nt // 2:], axis=1)], axis=0).astype(BF16)
        y = jnp.dot(t2_ref[...], stack, preferred_element_type=F32)
        for t in range(nt // 2):
            y_ref[t, pl.ds(k1, GRID_H, stride=pitch), :] = y[:, t * LANES:(t + 1) * LANES]

    for k2 in range(GRID_H):
        y = jnp.concatenate([y_ref[t, k2 * pitch:k2 * pitch + GRID_W, :] for t in range(nt // 2)], axis=1)
        sl = slice(k2 * GRID_W, (k2 + 1) * GRID_W)
        o_ref[sl, :] = (y * _silu(fg_ref[sl, :].astype(F32))).astype(BF16)


def _fourier_latent(h2d, p2d, fc, t1, t2, b):
    gd = FNET_GROUP_DIM
    n = GRID_H * GRID_W
    npad = GRID_H * FFT_PITCH
    cf = _OFF["f_gate"] // gd
    return pl.pallas_call(
        _fft_latent_kernel,
        grid=(b, FNET_GROUPS),
        in_specs=[pl.BlockSpec((n, gd), lambda bi, g: (bi, g)),
                  pl.BlockSpec((n, gd), lambda bi, g: (bi, cf + g)),
                  pl.BlockSpec((gd, 2 * gd), lambda bi, g: (0, 0)),
                  pl.BlockSpec((GRID_W, GRID_H, 2 * GRID_H), lambda bi, g: (0, 0, 0)),
                  pl.BlockSpec((GRID_H, 2 * GRID_W), lambda bi, g: (0, 0))],
        out_specs=pl.BlockSpec((n, gd), lambda bi, g: (bi, g)),
        out_shape=jax.ShapeDtypeStruct((b * n, D_MODEL), BF16),
        scratch_shapes=[pltpu.VMEM((2 * gd // LANES, npad, LANES), F32), pltpu.VMEM((2 * gd // LANES, npad, LANES), F32),
                        pltpu.VMEM((gd // LANES, npad, LANES), F32)],
        compiler_params=_params(("arbitrary", "arbitrary")),
        name="fourier_latent",
    )(h2d, p2d, fc, t1, t2)


def _merge_kernel(x_ref, mod_ref, mg_ref, u_ref, v_ref, sgate_ref, yd_ref, yn_ref, yf_ref,
                  gv_ref, sw_ref, sb_ref, psg_ref, pd_ref, pn_ref, pf_ref, wo_ref, gpost_ref,
                  o_ref, ysg_ref, *, latent, tm, tiles_per_batch):
    i = pl.program_id(0)
    lane = lax.broadcasted_iota(jnp.int32, (CHUNK, LANES), 1)
    for c in range(tm // CHUNK):
        sl = slice(c * CHUNK, (c + 1) * CHUNK)
        v = v_ref[sl, :].astype(F32)
        vn = (v * lax.rsqrt(jnp.mean(v * v, axis=-1, keepdims=True) + EPS) * gv_ref[...]).astype(BF16)
        parts = []
        for pr in range(SG_GROUPS // 2):
            vp = vn[:, pr * LANES:(pr + 1) * LANES]
            s0 = jnp.dot(sw_ref[2 * pr], vp, preferred_element_type=F32)
            s1 = jnp.dot(sw_ref[2 * pr + 1], vp, preferred_element_type=F32)
            parts.append(jnp.where(lane < SG_WIDTH // SG_GROUPS, s0, s1))
        s = jnp.concatenate(parts, axis=1) + sb_ref[...]
        y = u_ref[sl, :].astype(F32) * s * _silu(sgate_ref[sl, :].astype(F32))
        ysg_ref[sl, :] = y.astype(BF16)

    def gate(n):
        return jax.nn.sigmoid(mg_ref[:, n * D_MODEL:(n + 1) * D_MODEL].astype(F32))

    mixed = gate(0) * jnp.dot(ysg_ref[...], psg_ref[...], preferred_element_type=F32)
    mixed = mixed + gate(1) * jnp.dot(yd_ref[...], pd_ref[...], preferred_element_type=F32)
    mixed = mixed + gate(2) * jnp.dot(yn_ref[...], pn_ref[...], preferred_element_type=F32)
    mixed = mixed + gate(3) * jnp.dot(yf_ref[...], pf_ref[...], preferred_element_type=F32)
    out = jnp.dot(mixed.astype(BF16), wo_ref[...], preferred_element_type=F32)
    out = out * lax.rsqrt(jnp.mean(out * out, axis=-1, keepdims=True) + EPS) * gpost_ref[...]
    row = (1 + i // tiles_per_batch) if latent else 0
    g = mod_ref[pl.ds(row, 1), :][:, 2 * D_MODEL:]
    o_ref[...] = x_ref[...] + g * out


def _merge(x2d, mod_all, p2d, yd, yn, yf, w, l, *, latent, seq, tm=512):
    t = x2d.shape[0]
    cu, cv, cs = (_OFF[n] // SG_WIDTH for n in ("sg_u", "sg_v", "sg_gate"))
    tok = lambda width, c=0: pl.BlockSpec((tm, width), lambda i: (i, c))
    wspec = lambda *shape: pl.BlockSpec((None,) + shape, lambda i: (l,) + (0,) * len(shape))
    tiles_per_batch = seq // tm if latent else 1
    return pl.pallas_call(
        functools.partial(_merge_kernel, latent=latent, tm=tm, tiles_per_batch=tiles_per_batch),
        grid=(t // tm,),
        in_specs=[tok(D_MODEL), wspec(8, 3 * D_MODEL), tok(N_BRANCH * D_MODEL, 0),
                  tok(SG_WIDTH, cu), tok(SG_WIDTH, cv), tok(SG_WIDTH, cs),
                  tok(512), tok(512), tok(D_MODEL),
                  wspec(1, SG_WIDTH), wspec(SG_GROUPS, CHUNK, CHUNK), wspec(CHUNK, SG_WIDTH),
                  wspec(SG_WIDTH, D_MODEL), wspec(512, D_MODEL), wspec(512, D_MODEL),
                  wspec(D_MODEL, D_MODEL), wspec(D_MODEL, D_MODEL), wspec(1, D_MODEL)],
        out_specs=tok(D_MODEL),
        out_shape=jax.ShapeDtypeStruct((t, D_MODEL), F32),
        scratch_shapes=[pltpu.VMEM((tm, SG_WIDTH), BF16)],
        compiler_params=_params(("arbitrary",)),
        name="merge_latent" if latent else "merge_prompt",
    )(x2d, mod_all, p2d, p2d, p2d, p2d, yd, yn, yf,
      w["sg_norm_g"], w["sg_w"], w["sg_b"], w["p_sg"], w["p_diff"], w["p_na"], w["p_fnet"], w["w_out"],
      w["g_post"])


def _rope_tables(n_tokens):
    nf = DIFF_HEAD_DIM // 4
    t = np.arange(n_tokens)
    inv = ROPE_BASE ** (-np.arange(nf, dtype=np.float64) / nf)
    ang_r = (t // GRID_W).astype(np.float64)[:, None] * inv[None, :]
    ang_c = (t % GRID_W).astype(np.float64)[:, None] * inv[None, :]
    z = np.zeros_like(ang_r)
    cos64 = np.concatenate([np.cos(ang_r), np.cos(ang_r), np.cos(ang_c), np.cos(ang_c)], axis=1)
    sina64 = np.concatenate([-np.sin(ang_r), z, -np.sin(ang_c), z], axis=1)
    sinb64 = np.concatenate([z, np.sin(ang_r), z, np.sin(ang_c)], axis=1)
    two = lambda a: jnp.asarray(np.concatenate([a, a], axis=1).astype(np.float32))
    return two(cos64), two(sina64), two(sinb64)


def kernel(x_prompt, x_sample, cache_diff_k, cache_diff_v, cache_na_k, cache_na_v, c, c_ctx,
           w_mod, b_mod, g_pre, g_post, w_in, sg_norm_g, sg_w, sg_b,
           diff_lam_q1, diff_lam_k1, diff_lam_q2, diff_lam_k2, diff_subln_g, na_rpb,
           w_proj_sg, w_proj_diff, w_proj_na, w_proj_fnet, w_out):
    bp, sp, _ = x_prompt.shape
    bs, ss, _ = x_sample.shape
    n_ctx = cache_diff_k.shape[2]

    cond8 = jnp.zeros((8, D_MODEL), F32).at[0].set(c_ctx).at[1:1 + bs].set(c)
    mod_all = _modulation(cond8, w_mod, b_mod)

    w_in_b = w_in.astype(BF16)
    weights = {
        "sg_norm_g": sg_norm_g.reshape(DEPTH, 1, SG_WIDTH),
        "sg_w": sg_w.astype(BF16),
        "sg_b": jnp.repeat(jnp.transpose(sg_b, (0, 2, 1)), SG_WIDTH // SG_GROUPS, axis=2),
        "p_sg": w_proj_sg.astype(BF16), "p_diff": w_proj_diff.astype(BF16),
        "p_na": w_proj_na.astype(BF16), "p_fnet": w_proj_fnet.astype(BF16),
        "w_out": w_out.astype(BF16), "g_post": g_post.reshape(DEPTH, 1, D_MODEL),
    }
    g_pre3 = g_pre.reshape(DEPTH, 1, D_MODEL)
    lamp = jnp.stack([diff_lam_q1, diff_lam_k1, diff_lam_q2, diff_lam_k2], axis=1)
    subg3 = diff_subln_g.reshape(DEPTH, 1, 2 * DIFF_HEAD_DIM)
    rope_tabs = _rope_tables(ss)
    fc, fl, t1, t2 = _dft_tables()
    bias = _na_bias_tables(na_rpb)
    ckd = cache_diff_k.reshape(bs, DEPTH, n_ctx, DIFF_HEADS * 2 * DIFF_HEAD_DIM)
    cvd = cache_diff_v.reshape(bs, DEPTH, n_ctx, DIFF_HEADS * 2 * DIFF_HEAD_DIM)
    ckn = cache_na_k.reshape(bs, DEPTH, n_ctx, NA_HEADS * NA_HEAD_DIM)
    cvn = cache_na_v.reshape(bs, DEPTH, n_ctx, NA_HEADS * NA_HEAD_DIM)

    xp = x_prompt.reshape(bp * sp, D_MODEL)
    xs = x_sample.reshape(bs * ss, D_MODEL)
    caches = None
    for l in range(DEPTH):
        pp, hp, *caches = _inproj(xp, mod_all, g_pre3, w_in_b, l, latent=False, tm=4 * sp, seq=sp, n_col_tiles=7,
                                  cache_bufs=caches)
        pp3 = pp.reshape(bp, sp, IN_WIDTH)
        yd, yn, yf = (y.reshape(bp * sp, -1) for y in _prompt_mixers(pp3, hp, lamp, subg3, fc, fl, l))
        xp = _merge(xp, mod_all, pp, yd, yn, yf, weights, l, latent=False, seq=sp)
        ps, hs = _inproj(xs, mod_all, g_pre3, w_in_b, l, latent=True, tm=1024, seq=ss, n_col_tiles=3,
                         rope_tabs=rope_tabs)
        ps3 = ps.reshape(bs, ss, IN_WIDTH)
        yd = _diff_attention(ps3, lamp, subg3, l, tq=2048, heads=1, sub=8, ctx=(ckd, cvd)).reshape(bs * ss, -1)
        yn = _neighbourhood_attention(ps3, ckn, cvn, bias, l).reshape(bs * ss, -1)
        yf = _fourier_latent(hs, ps, fc, t1, t2, bs)
        xs = _merge(xs, mod_all, ps, yd, yn, yf, weights, l, latent=True, seq=ss)

    dk5, dv5, nk5, nv5 = caches
    return (xp.reshape(bp, sp, D_MODEL), xs.reshape(bs, ss, D_MODEL),
            dk5, dv5,
            nk5.reshape(bp, DEPTH, sp, NA_HEADS, NA_HEAD_DIM),
            nv5.reshape(bp, DEPTH, sp, NA_HEADS, NA_HEAD_DIM))
```
